```python
import math
import jax, jax.numpy as jnp
from jax import lax
import numpy as np

D_MODEL = 1024
BATCH = 8
SEQ = 4096
DEPTH = 2

PLE_DIM = 256
FOX_HEADS = 8
FOX_HEAD_DIM = 64
FOX_WIDTH = FOX_HEADS * FOX_HEAD_DIM
Q_BLOCK = 128
POOL_WINDOWS = (2, 4, 8, 16)
POOL_GROUPS = 4
POOL_GROUP_DIM = 128
POOL_WIDTH = POOL_GROUPS * POOL_GROUP_DIM
GDN_HEADS = 4
GDN_HEAD_DIM = 128
GDN_WIDTH = GDN_HEADS * GDN_HEAD_DIM
GDN_CONV = 4
GDN_CHUNK = 64
N_BRANCHES = 3
IN_SPLITS = (3 * FOX_WIDTH, FOX_HEADS, POOL_WIDTH, 3 * GDN_WIDTH, GDN_HEADS, GDN_HEADS, GDN_WIDTH, N_BRANCHES * D_MODEL)
IN_WIDTH = sum(IN_SPLITS)
PEER_HEADS = 8
PEER_KEYS = 128
PEER_N_EXPERTS = PEER_KEYS * PEER_KEYS
PEER_QUERY_DIM = 256
PEER_HALF = PEER_QUERY_DIM // 2
PEER_TOPK = 16
PEER_TOKEN_BLOCK = 128
DEEPNORM_ALPHA = (2 * DEPTH) ** 0.25
DEEPNORM_BETA = (8 * DEPTH) ** -0.25
LN_EPS = 1e-5
RMS_EPS = 1e-6
L2_EPS = 1e-6

kernel_name = "hybrid_fox_pool_gdn_peer_deepnorm"


def layer_norm(x, g, b):
    xf = x.astype(jnp.float32)
    mu = jnp.mean(xf, axis=-1, keepdims=True)
    var = jnp.mean(jnp.square(xf - mu), axis=-1, keepdims=True)
    return ((xf - mu) * lax.rsqrt(var + LN_EPS) * g.astype(jnp.float32) + b.astype(jnp.float32)).astype(x.dtype)


def forgetting_attention(q, k, v, f_logit):
    B, S, H, Dh = q.shape
    nb = S // Q_BLOCK
    cum = jnp.cumsum(jax.nn.log_sigmoid(f_logit.astype(jnp.float32)), axis=1)
    cum_k = cum.transpose(0, 2, 1)[:, :, None, :]
    kpos = jnp.arange(S)
    scale = FOX_HEAD_DIM ** -0.5
    qb = q.reshape(B, nb, Q_BLOCK, H, Dh).transpose(1, 0, 2, 3, 4)
    cb = cum.reshape(B, nb, Q_BLOCK, H).transpose(1, 0, 3, 2)
    starts = jnp.arange(nb) * Q_BLOCK

    def block(args):
        qi, ci, start = args
        s = jnp.einsum('bqhd,bkhd->bhqk', qi, k).astype(jnp.float32) * scale
        s = s + ci[..., None] - cum_k
        qpos = start + jnp.arange(Q_BLOCK)
        causal = qpos[:, None] >= kpos[None, :]
        s = jnp.where(causal, s, -jnp.inf)
        pr = jax.nn.softmax(s, axis=-1).astype(v.dtype)
        return jnp.einsum('bhqk,bkhd->bqhd', pr, v)

    out = lax.map(block, (qb, cb, starts))
    return out.transpose(1, 0, 2, 3, 4).reshape(B, S, H * Dh)


def multiscale_pool(xp, w_pool, pool_scale):
    B, S, _ = xp.shape
    xg = xp.reshape(B, S, POOL_GROUPS, POOL_GROUP_DIM).astype(jnp.float32)
    cs = jnp.concatenate([jnp.zeros_like(xg[:, :1]), jnp.cumsum(xg, axis=1)], axis=1)
    t = jnp.arange(S)
    windows = jnp.array(POOL_WINDOWS, dtype=jnp.int32)
    lo = jnp.maximum(t[:, None] + 1 - windows[None, :], 0)
    grp = jnp.arange(POOL_GROUPS)[None, :]
    win_sum = cs[:, 1:] - cs[:, lo, grp, :]
    count = (t[:, None] + 1 - lo).astype(jnp.float32)
    pooled = win_sum / count[None, :, :, None] - xg
    mixed = jnp.einsum('bsgc,gcd->bsgd', pooled, w_pool.astype(jnp.float32))
    out = mixed.reshape(B, S, POOL_WIDTH) * pool_scale.astype(jnp.float32)
    return out.astype(xp.dtype)


def causal_depthwise_conv(x, w):
    C = x.shape[-1]
    return lax.conv_general_dilated(x, w.astype(x.dtype), window_strides=(1,), padding=[(GDN_CONV - 1, 0)],
                                    dimension_numbers=('NWC', 'WIO', 'NWC'), feature_group_count=C)


def l2norm(x):
    return x * lax.rsqrt(jnp.sum(x * x, axis=-1, keepdims=True) + L2_EPS)


def gated_delta_rule(q, k, v, g, beta):
    B, S, H, D = q.shape
    C = GDN_CHUNK
    n = S // C

    def chunks(a):
        a = a.reshape((B, n, C, H) + a.shape[3:])
        return jnp.moveaxis(a, 3, 1)

    q = chunks(q) * (D ** -0.5)
    k = chunks(k)
    v = chunks(v)
    g = jnp.cumsum(chunks(g), axis=-1)
    beta = chunks(beta)
    k_beta = k * beta[..., None]
    v_beta = v * beta[..., None]
    tri_strict = jnp.tril(jnp.ones((C, C), dtype=bool), -1)
    tri_incl = jnp.tril(jnp.ones((C, C), dtype=bool))

    def decay(mask):
        return jnp.exp(jnp.where(mask, g[..., :, None] - g[..., None, :], -jnp.inf))

    L = jnp.einsum('bhnid,bhnjd->bhnij', k_beta, k) * decay(tri_strict)
    eye = jnp.eye(C, dtype=jnp.float32)
    T = lax.linalg.triangular_solve(eye + L, jnp.broadcast_to(eye, L.shape), left_side=True, lower=True,
                                    unit_diagonal=True)
    u = T @ v_beta
    w = T @ (k_beta * jnp.exp(g)[..., None])
    a_intra = jnp.einsum('bhnid,bhnjd->bhnij', q, k) * decay(tri_incl)
    g_last = g[..., -1]

    def step(state, xs):
        q_c, k_c, u_c, w_c, a_c, g_c, gl = xs
        v_new = u_c - w_c @ state
        o = (q_c * jnp.exp(g_c)[..., None]) @ state + a_c @ v_new
        k_dec = k_c * jnp.exp(gl[..., None] - g_c)[..., None]
        state = state * jnp.exp(gl)[..., None, None] + jnp.einsum('bhcd,bhce->bhde', k_dec, v_new)
        return state, o

    xs = tuple(jnp.moveaxis(a, 2, 0) for a in (q, k, u, w, a_intra, g, g_last))
    s0 = jnp.zeros((B, H, D, D), dtype=jnp.float32)
    _, o = lax.scan(step, s0, xs)
    return o.transpose(1, 0, 3, 2, 4).reshape(B, S, H, D)


def gated_deltanet(qkv, b_logit, a_logit, z, conv_w, a_log, dt_bias, norm_g):
    B, S, _ = qkv.shape
    act = jax.nn.silu(causal_depthwise_conv(qkv, conv_w)).astype(jnp.float32)
    q, k, v = jnp.split(act, 3, axis=-1)
    q = l2norm(q.reshape(B, S, GDN_HEADS, GDN_HEAD_DIM))
    k = l2norm(k.reshape(B, S, GDN_HEADS, GDN_HEAD_DIM))
    v = v.reshape(B, S, GDN_HEADS, GDN_HEAD_DIM)
    beta = jax.nn.sigmoid(b_logit.astype(jnp.float32))
    g = -jnp.exp(a_log.astype(jnp.float32)) * jax.nn.softplus(a_logit.astype(jnp.float32) + dt_bias.astype(jnp.float32))
    o = gated_delta_rule(q, k, v, g, beta)
    o = o * lax.rsqrt(jnp.mean(o * o, axis=-1, keepdims=True) + RMS_EPS) * norm_g.astype(jnp.float32)
    o = o * jax.nn.silu(z.astype(jnp.float32).reshape(B, S, GDN_HEADS, GDN_HEAD_DIM))
    return o.reshape(B, S, GDN_WIDTH).astype(qkv.dtype)


def token_mixer(h, w_in, fox_fb, pool_w, pool_scale, gdn_conv, gdn_a_log, gdn_dt_bias, gdn_norm, w_branch, w_out):
    B, S, _ = h.shape
    proj = h @ w_in
    offsets = [int(o) for o in np.cumsum(IN_SPLITS)[:-1]]
    fox_qkv, fox_f, pool_in, gdn_qkv, gdn_b, gdn_a, gdn_z, gates = jnp.split(proj, offsets, axis=-1)
    fq, fk, fv = [t.reshape(B, S, FOX_HEADS, FOX_HEAD_DIM) for t in jnp.split(fox_qkv, 3, axis=-1)]
    branch_a = forgetting_attention(fq, fk, fv, fox_f + fox_fb)
    branch_b = multiscale_pool(pool_in, pool_w, pool_scale)
    branch_c = gated_deltanet(gdn_qkv, gdn_b, gdn_a, gdn_z, gdn_conv, gdn_a_log, gdn_dt_bias, gdn_norm)
    branches = jnp.stack([branch_a, branch_b, branch_c], axis=2)
    to_model = jnp.einsum('bsnw,nwd->bsnd', branches, w_branch)
    gate = jax.nn.sigmoid(gates.reshape(B, S, N_BRANCHES, D_MODEL))
    merged = jnp.sum(gate * to_model, axis=2)
    return merged @ w_out


def peer_ffn(h, w_q, sub_k1, sub_k2, u_tab, v_tab):
    B, S, D = h.shape
    q = (h @ w_q).reshape(B, S, PEER_HEADS, 2, PEER_HALF)
    s1 = jnp.einsum('bshd,hnd->bshn', q[..., 0, :], sub_k1).astype(jnp.float32)
    s2 = jnp.einsum('bshd,hnd->bshn', q[..., 1, :], sub_k2).astype(jnp.float32)
    v1, i1 = lax.top_k(s1, PEER_TOPK)
    v2, i2 = lax.top_k(s2, PEER_TOPK)
    cand_s = (v1[..., :, None] + v2[..., None, :]).reshape(B, S, PEER_HEADS, PEER_TOPK * PEER_TOPK)
    cand_i = (i1[..., :, None] * PEER_KEYS + i2[..., None, :]).reshape(B, S, PEER_HEADS, PEER_TOPK * PEER_TOPK)
    top_s, pos = lax.top_k(cand_s, PEER_TOPK)
    idx = jnp.take_along_axis(cand_i, pos, axis=-1)
    wts = jax.nn.softmax(top_s, axis=-1).astype(h.dtype)
    n_blk = (B * S) // PEER_TOKEN_BLOCK
    xt = h.reshape(n_blk, PEER_TOKEN_BLOCK, D)
    it = idx.reshape(n_blk, PEER_TOKEN_BLOCK, PEER_HEADS, PEER_TOPK)
    wt = wts.reshape(n_blk, PEER_TOKEN_BLOCK, PEER_HEADS, PEER_TOPK)

    def block(args):
        xb, ib, wb = args
        u_sel = u_tab[ib]
        act = jax.nn.gelu(jnp.einsum('thkd,td->thk', u_sel, xb), approximate=False)
        return jnp.einsum('thk,thkd->td', act * wb, v_tab[ib])

    out = lax.map(block, (xt, it, wt))
    return out.reshape(B, S, D)


def setup_inputs(seed: int = 0) -> dict:
    key = jax.random.key(seed)
    ks = jax.random.split(key, 26)
    f32 = jnp.float32
    nrm = lambda k, shape, s: jax.random.normal(k, shape, f32) * s
    dt = jnp.exp(jax.random.uniform(ks[9], (DEPTH, GDN_HEADS), f32, math.log(1e-3), math.log(1e-1)))
    return {
        "x": nrm(ks[0], (BATCH, SEQ, D_MODEL), 1.0),
        "p": nrm(ks[1], (DEPTH, BATCH, SEQ, PLE_DIM), 1.0),
        "w_in": nrm(ks[2], (DEPTH, D_MODEL, IN_WIDTH), D_MODEL ** -0.5),
        "fox_fb": 3.0 + nrm(ks[3], (DEPTH, FOX_HEADS), 0.5),
        "pool_w": nrm(ks[4], (DEPTH, POOL_GROUPS, POOL_GROUP_DIM, POOL_GROUP_DIM), POOL_GROUP_DIM ** -0.5),
        "pool_scale": 1.0 + nrm(ks[5], (DEPTH, POOL_WIDTH), 0.1),
        "gdn_conv": nrm(ks[6], (DEPTH, GDN_CONV, 1, 3 * GDN_WIDTH), GDN_CONV ** -0.5),
        "gdn_a_log": jnp.log(jax.random.uniform(ks[7], (DEPTH, GDN_HEADS), f32, 1.0, 16.0)),
        "gdn_dt_bias": dt + jnp.log(-jnp.expm1(-dt)),
        "gdn_norm": 1.0 + nrm(ks[8], (DEPTH, GDN_HEAD_DIM), 0.05),
        "w_branch": nrm(ks[10], (DEPTH, N_BRANCHES, FOX_WIDTH, D_MODEL), FOX_WIDTH ** -0.5),
        "w_out": nrm(ks[11], (DEPTH, D_MODEL, D_MODEL), D_MODEL ** -0.5 * DEEPNORM_BETA),
        "ln1_g": 1.0 + nrm(ks[12], (DEPTH, D_MODEL), 0.05),
        "ln1_b": nrm(ks[13], (DEPTH, D_MODEL), 0.02),
        "peer_wq": nrm(ks[14], (DEPTH, D_MODEL, PEER_HEADS * PEER_QUERY_DIM), D_MODEL ** -0.5),
        "peer_k1": nrm(ks[15], (DEPTH, PEER_HEADS, PEER_KEYS, PEER_HALF), PEER_HALF ** -0.5),
        "peer_k2": nrm(ks[16], (DEPTH, PEER_HEADS, PEER_KEYS, PEER_HALF), PEER_HALF ** -0.5),
        "peer_u": nrm(ks[17], (DEPTH, PEER_N_EXPERTS, D_MODEL), D_MODEL ** -0.5),
        "peer_v": nrm(ks[18], (DEPTH, PEER_N_EXPERTS, D_MODEL), DEEPNORM_BETA),
        "ple_gate": nrm(ks[19], (DEPTH, D_MODEL, D_MODEL), D_MODEL ** -0.5),
        "ple_proj": nrm(ks[20], (DEPTH, PLE_DIM, D_MODEL), PLE_DIM ** -0.5 * DEEPNORM_BETA),
        "ln2_g": 1.0 + nrm(ks[21], (DEPTH, D_MODEL), 0.05),
        "ln2_b": nrm(ks[22], (DEPTH, D_MODEL), 0.02),
    }


def reference(x, p, w_in, fox_fb, pool_w, pool_scale, gdn_conv, gdn_a_log, gdn_dt_bias, gdn_norm, w_branch, w_out,
              ln1_g, ln1_b, peer_wq, peer_k1, peer_k2, peer_u, peer_v, ple_gate, ple_proj, ln2_g, ln2_b):
    h = x
    for i in range(DEPTH):
        mix = token_mixer(h, w_in[i], fox_fb[i], pool_w[i], pool_scale[i], gdn_conv[i], gdn_a_log[i],
                          gdn_dt_bias[i], gdn_norm[i], w_branch[i], w_out[i])
        h = layer_norm(DEEPNORM_ALPHA * h + mix, ln1_g[i], ln1_b[i])
        ffn = peer_ffn(h, peer_wq[i], peer_k1[i], peer_k2[i], peer_u[i], peer_v[i])
        ple = jax.nn.sigmoid(h @ ple_gate[i]) * (p[i] @ ple_proj[i])
        h = layer_norm(DEEPNORM_ALPHA * h + ffn + ple, ln2_g[i], ln2_b[i])
    return h
```

```python
import functools
import math

import jax
import jax.numpy as jnp
import numpy as np
from jax import lax
from jax.experimental import pallas as pl
from jax.experimental.pallas import tpu as pltpu

D_MODEL = 1024
DEPTH = 2
FOX_HEADS = 8
FOX_HEAD_DIM = 64
FOX_WIDTH = FOX_HEADS * FOX_HEAD_DIM
Q_BLOCK = 128
POOL_WINDOWS = (2, 4, 8, 16)
POOL_GROUPS = 4
POOL_GROUP_DIM = 128
POOL_WIDTH = POOL_GROUPS * POOL_GROUP_DIM
GDN_HEADS = 4
GDN_HEAD_DIM = 128
GDN_WIDTH = GDN_HEADS * GDN_HEAD_DIM
GDN_CONV = 4
GDN_CHUNK = 64
N_BRANCHES = 3
IN_SPLITS = (3 * FOX_WIDTH, FOX_HEADS, POOL_WIDTH, 3 * GDN_WIDTH, GDN_HEADS, GDN_HEADS, GDN_WIDTH, N_BRANCHES * D_MODEL)
PEER_HEADS = 8
PEER_KEYS = 128
PEER_N_EXPERTS = PEER_KEYS * PEER_KEYS
PEER_QUERY_DIM = 256
PEER_HALF = PEER_QUERY_DIM // 2
PEER_TOPK = 16
PEER_PAIRS = PEER_HEADS * PEER_TOPK
DEEPNORM_ALPHA = (2 * DEPTH) ** 0.25
LN_EPS = 1e-5
RMS_EPS = 1e-6
L2_EPS = 1e-6

LANES = 128
SUBLANES = 8
VMEM_LIMIT_BYTES = 52 * 1024 * 1024

PEER_TOKEN_TILE = 128
ROW_WORDS = D_MODEL // 2 // LANES
PLANE_STRIDE = PEER_PAIRS + 1
PLANE_ROWS = (ROW_WORDS - 1) * PLANE_STRIDE + PEER_PAIRS


def _pack_table(tab):
    e = tab.shape[0]
    b = lax.bitcast_convert_type(tab.astype(jnp.bfloat16), jnp.uint16).astype(jnp.uint32)
    b = b.reshape(e, 2, ROW_WORDS, LANES)
    w = b[:, 0] | (b[:, 1] << 16)
    return lax.bitcast_convert_type(w, jnp.int32).reshape(e * ROW_WORDS, LANES)


def _unpack_lo(w):
    return lax.bitcast_convert_type(w << 16, jnp.float32)


def _unpack_hi(w):
    return lax.bitcast_convert_type(w & jnp.int32(-65536), jnp.float32)


def _gather_rows(idx_ref, tab_ref, tile_ref, t):
    for p in range(PEER_PAIRS):
        row = pl.multiple_of(idx_ref[p, t], ROW_WORDS)
        tile_ref[pl.ds(p, ROW_WORDS, stride=PLANE_STRIDE), :] = tab_ref[pl.ds(row, ROW_WORDS), :]


def _peer_u_kernel(idx_ref, x_ref, wt_ref, tab_ref, ct_ref, tile_ref):
    lane = lax.broadcasted_iota(jnp.int32, (PEER_PAIRS, PEER_TOKEN_TILE), 1)

    def token(t, act_t):
        _gather_rows(idx_ref, tab_ref, tile_ref, t)
        x = x_ref[t]
        acc = jnp.zeros((PEER_PAIRS, LANES), jnp.float32)
        for s in range(ROW_WORDS):
            w = tile_ref[pl.ds(s * PLANE_STRIDE, PEER_PAIRS), :]
            acc = acc + _unpack_lo(w) * x[s:s + 1, :] + _unpack_hi(w) * x[s + ROW_WORDS:s + ROW_WORDS + 1, :]
        a = jnp.sum(acc, axis=1, keepdims=True)
        return jnp.where(lane == t, a, act_t)

    act_t = lax.fori_loop(0, PEER_TOKEN_TILE, token, jnp.zeros((PEER_PAIRS, PEER_TOKEN_TILE), jnp.float32))
    gelu = 0.5 * act_t * (1.0 + lax.erf(act_t * (2.0 ** -0.5)))
    ct_ref[...] = gelu * wt_ref[...]


def _peer_v_kernel(idx_ref, ct_ref, tab_ref, out_ref, tile_ref):
    lane = lax.broadcasted_iota(jnp.int32, (PEER_PAIRS, PEER_TOKEN_TILE), 1)

    def token(t, carry):
        _gather_rows(idx_ref, tab_ref, tile_ref, t)
        c = jnp.sum(jnp.where(lane == t, ct_ref[...], 0.0), axis=1, keepdims=True)
        lo, hi = [], []
        for s in range(ROW_WORDS):
            w = tile_ref[pl.ds(s * PLANE_STRIDE, PEER_PAIRS), :]
            lo.append(jnp.sum(_unpack_lo(w) * c, axis=0, keepdims=True))
            hi.append(jnp.sum(_unpack_hi(w) * c, axis=0, keepdims=True))
        out_ref[t] = jnp.concatenate(lo + hi, axis=0)
        return carry

    lax.fori_loop(0, PEER_TOKEN_TILE, token, 0)


def _resident_spec(shape):
    return pl.BlockSpec(shape, lambda i: (0,) * len(shape), pipeline_mode=pl.Buffered(1))


def _peer_retrieve(x, idx_t, wts_t, u_packed, v_packed):
    n = x.shape[0]
    grid = (n // PEER_TOKEN_TILE,)
    params = pltpu.CompilerParams(dimension_semantics=("arbitrary",), vmem_limit_bytes=VMEM_LIMIT_BYTES)
    idx_spec = pl.BlockSpec((PEER_PAIRS, PEER_TOKEN_TILE), lambda i: (0, i), memory_space=pltpu.SMEM)
    pair_spec = pl.BlockSpec((PEER_PAIRS, PEER_TOKEN_TILE), lambda i: (0, i))
    row_spec = pl.BlockSpec((PEER_TOKEN_TILE, SUBLANES, LANES), lambda i: (i, 0, 0))
    tile = pltpu.VMEM((PLANE_ROWS, LANES), jnp.int32)
    ct = pl.pallas_call(
        _peer_u_kernel,
        grid=grid,
        in_specs=[idx_spec, row_spec, pair_spec, _resident_spec(u_packed.shape)],
        out_specs=pair_spec,
        out_shape=jax.ShapeDtypeStruct((PEER_PAIRS, n), jnp.float32),
        scratch_shapes=[tile],
        compiler_params=params,
        name="peer_u",
    )(idx_t, x.reshape(n, SUBLANES, LANES), wts_t, u_packed)
    out = pl.pallas_call(
        _peer_v_kernel,
        grid=grid,
        in_specs=[idx_spec, pair_spec, _resident_spec(v_packed.shape)],
        out_specs=row_spec,
        out_shape=jax.ShapeDtypeStruct((n, SUBLANES, LANES), jnp.float32),
        scratch_shapes=[tile],
        compiler_params=params,
        name="peer_v",
    )(idx_t, ct, v_packed)
    return out.reshape(n, D_MODEL)


def _layer_norm(x, g, b):
    mu = jnp.mean(x, axis=-1, keepdims=True)
    var = jnp.mean(jnp.square(x - mu), axis=-1, keepdims=True)
    return (x - mu) * lax.rsqrt(var + LN_EPS) * g + b


def _forgetting_attention(q, k, v, f_logit):
    B, S, H, Dh = q.shape
    nb = S // Q_BLOCK
    cum = jnp.cumsum(jax.nn.log_sigmoid(f_logit), axis=1)
    cum_k = cum.transpose(0, 2, 1)[:, :, None, :]
    kpos = jnp.arange(S)
    scale = FOX_HEAD_DIM ** -0.5
    qb = q.reshape(B, nb, Q_BLOCK, H, Dh).transpose(1, 0, 2, 3, 4)
    cb = cum.reshape(B, nb, Q_BLOCK, H).transpose(1, 0, 3, 2)
    starts = jnp.arange(nb) * Q_BLOCK

    def block(args):
        qi, ci, start = args
        s = jnp.einsum('bqhd,bkhd->bhqk', qi, k) * scale
        s = s + ci[..., None] - cum_k
        qpos = start + jnp.arange(Q_BLOCK)
        s = jnp.where(qpos[:, None] >= kpos[None, :], s, -jnp.inf)
        pr = jax.nn.softmax(s, axis=-1)
        return jnp.einsum('bhqk,bkhd->bqhd', pr, v)

    out = lax.map(block, (qb, cb, starts))
    return out.transpose(1, 0, 2, 3, 4).reshape(B, S, H * Dh)


def _multiscale_pool(xp, w_pool, pool_scale):
    B, S, _ = xp.shape
    xg = xp.reshape(B, S, POOL_GROUPS, POOL_GROUP_DIM)
    cs = jnp.concatenate([jnp.zeros_like(xg[:, :1]), jnp.cumsum(xg, axis=1)], axis=1)
    t = jnp.arange(S)
    windows = jnp.array(POOL_WINDOWS, dtype=jnp.int32)
    lo = jnp.maximum(t[:, None] + 1 - windows[None, :], 0)
    grp = jnp.arange(POOL_GROUPS)[None, :]
    win_sum = cs[:, 1:] - cs[:, lo, grp, :]
    count = (t[:, None] + 1 - lo).astype(jnp.float32)
    pooled = win_sum / count[None, :, :, None] - xg
    mixed = jnp.einsum('bsgc,gcd->bsgd', pooled, w_pool)
    return mixed.reshape(B, S, POOL_WIDTH) * pool_scale


def _l2norm(x):
    return x * lax.rsqrt(jnp.sum(x * x, axis=-1, keepdims=True) + L2_EPS)


def _gated_delta_rule(q, k, v, g, beta):
    B, S, H, D = q.shape
    C = GDN_CHUNK
    n = S // C

    def chunks(a):
        a = a.reshape((B, n, C, H) + a.shape[3:])
        return jnp.moveaxis(a, 3, 1)

    q = chunks(q) * (D ** -0.5)
    k = chunks(k)
    v = chunks(v)
    g = jnp.cumsum(chunks(g), axis=-1)
    beta = chunks(beta)
    k_beta = k * beta[..., None]
    v_beta = v * beta[..., None]
    tri_strict = jnp.tril(jnp.ones((C, C), dtype=bool), -1)
    tri_incl = jnp.tril(jnp.ones((C, C), dtype=bool))

    def decay(mask):
        return jnp.exp(jnp.where(mask, g[..., :, None] - g[..., None, :], -jnp.inf))

    L = jnp.einsum('bhnid,bhnjd->bhnij', k_beta, k) * decay(tri_strict)
    eye = jnp.eye(C, dtype=jnp.float32)
    T = lax.linalg.triangular_solve(eye + L, jnp.broadcast_to(eye, L.shape), left_side=True, lower=True, unit_diagonal=True)
    u = T @ v_beta
    w = T @ (k_beta * jnp.exp(g)[..., None])
    a_intra = jnp.einsum('bhnid,bhnjd->bhnij', q, k) * decay(tri_incl)
    g_last = g[..., -1]

    def step(state, xs):
        q_c, k_c, u_c, w_c, a_c, g_c, gl = xs
        v_new = u_c - w_c @ state
        o = (q_c * jnp.exp(g_c)[..., None]) @ state + a_c @ v_new
        k_dec = k_c * jnp.exp(gl[..., None] - g_c)[..., None]
        state = state * jnp.exp(gl)[..., None, None] + jnp.einsum('bhcd,bhce->bhde', k_dec, v_new)
        return state, o

    xs = tuple(jnp.moveaxis(a, 2, 0) for a in (q, k, u, w, a_intra, g, g_last))
    s0 = jnp.zeros((B, H, D, D), dtype=jnp.float32)
    _, o = lax.scan(step, s0, xs)
    return o.transpose(1, 0, 3, 2, 4).reshape(B, S, H, D)


def _gated_deltanet(qkv, b_logit, a_logit, z, conv_w, a_log, dt_bias, norm_g):
    B, S, C = qkv.shape
    act = jax.nn.silu(lax.conv_general_dilated(qkv, conv_w, window_strides=(1,), padding=[(GDN_CONV - 1, 0)],
                                               dimension_numbers=('NWC', 'WIO', 'NWC'), feature_group_count=C))
    q, k, v = jnp.split(act, 3, axis=-1)
    q = _l2norm(q.reshape(B, S, GDN_HEADS, GDN_HEAD_DIM))
    k = _l2norm(k.reshape(B, S, GDN_HEADS, GDN_HEAD_DIM))
    v = v.reshape(B, S, GDN_HEADS, GDN_HEAD_DIM)
    beta = jax.nn.sigmoid(b_logit)
    g = -jnp.exp(a_log) * jax.nn.softplus(a_logit + dt_bias)
    o = _gated_delta_rule(q, k, v, g, beta)
    o = o * lax.rsqrt(jnp.mean(o * o, axis=-1, keepdims=True) + RMS_EPS) * norm_g
    o = o * jax.nn.silu(z.reshape(B, S, GDN_HEADS, GDN_HEAD_DIM))
    return o.reshape(B, S, GDN_WIDTH)


def _token_mixer(h, w_in, fox_fb, pool_w, pool_scale, gdn_conv, gdn_a_log, gdn_dt_bias, gdn_norm, w_branch, w_out):
    B, S, _ = h.shape
    proj = h @ w_in
    offsets = [int(o) for o in np.cumsum(IN_SPLITS)[:-1]]
    fox_qkv, fox_f, pool_in, gdn_qkv, gdn_b, gdn_a, gdn_z, gates = jnp.split(proj, offsets, axis=-1)
    fq, fk, fv = [t.reshape(B, S, FOX_HEADS, FOX_HEAD_DIM) for t in jnp.split(fox_qkv, 3, axis=-1)]
    branch_a = _forgetting_attention(fq, fk, fv, fox_f + fox_fb)
    branch_b = _multiscale_pool(pool_in, pool_w, pool_scale)
    branch_c = _gated_deltanet(gdn_qkv, gdn_b, gdn_a, gdn_z, gdn_conv, gdn_a_log, gdn_dt_bias, gdn_norm)
    branches = jnp.stack([branch_a, branch_b, branch_c], axis=2)
    to_model = jnp.einsum('bsnw,nwd->bsnd', branches, w_branch)
    gate = jax.nn.sigmoid(gates.reshape(B, S, N_BRANCHES, D_MODEL))
    merged = jnp.sum(gate * to_model, axis=2)
    return merged @ w_out


def _peer_select(h2, w_q, sub_k1, sub_k2):
    n = h2.shape[0]
    q = (h2 @ w_q).reshape(n, PEER_HEADS, 2, PEER_HALF)
    s1 = jnp.einsum('nhd,hkd->nhk', q[:, :, 0, :], sub_k1)
    s2 = jnp.einsum('nhd,hkd->nhk', q[:, :, 1, :], sub_k2)
    v1, i1 = lax.top_k(s1, PEER_TOPK)
    v2, i2 = lax.top_k(s2, PEER_TOPK)
    cand_s = (v1[..., :, None] + v2[..., None, :]).reshape(n, PEER_HEADS, PEER_TOPK * PEER_TOPK)
    cand_i = (i1[..., :, None] * PEER_KEYS + i2[..., None, :]).reshape(n, PEER_HEADS, PEER_TOPK * PEER_TOPK)
    top_s, pos = lax.top_k(cand_s, PEER_TOPK)
    idx = jnp.take_along_axis(cand_i, pos, axis=-1)
    wts = jax.nn.softmax(top_s, axis=-1)
    idx_t = (idx.reshape(n, PEER_PAIRS) * ROW_WORDS).astype(jnp.int32).T
    wts_t = wts.reshape(n, PEER_PAIRS).T
    return idx_t, wts_t


def kernel(x, p, w_in, fox_fb, pool_w, pool_scale, gdn_conv, gdn_a_log, gdn_dt_bias, gdn_norm, w_branch, w_out, ln1_g, ln1_b, peer_wq, peer_k1, peer_k2, peer_u, peer_v, ple_gate, ple_proj, ln2_g, ln2_b):
    B, S, D = x.shape
    h = x
    for i in range(DEPTH):
        mix = _token_mixer(h, w_in[i], fox_fb[i], pool_w[i], pool_scale[i], gdn_conv[i], gdn_a_log[i],
                           gdn_dt_bias[i], gdn_norm[i], w_branch[i], w_out[i])
        h = _layer_norm(DEEPNORM_ALPHA * h + mix, ln1_g[i], ln1_b[i])
        h2 = h.reshape(B * S, D)
        idx_t, wts_t = _peer_select(h2, peer_wq[i], peer_k1[i], peer_k2[i])
        ffn = _peer_retrieve(h2, idx_t, wts_t, _pack_table(peer_u[i]), _pack_table(peer_v[i])).reshape(B, S, D)
        ple = jax.nn.sigmoid(h @ ple_gate[i]) * (p[i] @ ple_proj[i])
        h = _layer_norm(DEEPNORM_ALPHA * h + ffn + ple, ln2_g[i], ln2_b[i])
    return h
```

```python
import functools
import math

import jax
import jax.numpy as jnp
import numpy as np
from jax import lax
from jax.experimental import pallas as pl
from jax.experimental.pallas import tpu as pltpu

D_MODEL = 1024
DEPTH = 2
FOX_HEADS = 8
FOX_HEAD_DIM = 64
FOX_WIDTH = FOX_HEADS * FOX_HEAD_DIM
Q_BLOCK = 128
POOL_WINDOWS = (2, 4, 8, 16)
POOL_GROUPS = 4
POOL_GROUP_DIM = 128
POOL_WIDTH = POOL_GROUPS * POOL_GROUP_DIM
GDN_HEADS = 4
GDN_HEAD_DIM = 128
GDN_WIDTH = GDN_HEADS * GDN_HEAD_DIM
GDN_CONV = 4
GDN_CHUNK = 64
N_BRANCHES = 3
IN_SPLITS = (3 * FOX_WIDTH, FOX_HEADS, POOL_WIDTH, 3 * GDN_WIDTH, GDN_HEADS, GDN_HEADS, GDN_WIDTH, N_BRANCHES * D_MODEL)
PEER_HEADS = 8
PEER_KEYS = 128
PEER_N_EXPERTS = PEER_KEYS * PEER_KEYS
PEER_QUERY_DIM = 256
PEER_HALF = PEER_QUERY_DIM // 2
PEER_TOPK = 16
PEER_PAIRS = PEER_HEADS * PEER_TOPK
DEEPNORM_ALPHA = (2 * DEPTH) ** 0.25
LN_EPS = 1e-5
RMS_EPS = 1e-6
L2_EPS = 1e-6

LANES = 128
SUBLANES = 8
VMEM_LIMIT_BYTES = 52 * 1024 * 1024

PEER_TOKEN_TILE = 128
ROW_WORDS = D_MODEL // 2 // LANES
PLANE_STRIDE = PEER_PAIRS + 1
PLANE_ROWS = (ROW_WORDS - 1) * PLANE_STRIDE + PEER_PAIRS


def _pack_table(tab):
    e = tab.shape[0]
    b = lax.bitcast_convert_type(tab.astype(jnp.bfloat16), jnp.uint16).astype(jnp.uint32)
    b = b.reshape(e, 2, ROW_WORDS, LANES)
    w = b[:, 0] | (b[:, 1] << 16)
    return lax.bitcast_convert_type(w, jnp.int32).reshape(e * ROW_WORDS, LANES)


def _unpack_lo(w):
    return lax.bitcast_convert_type(w << 16, jnp.float32)


def _unpack_hi(w):
    return lax.bitcast_convert_type(w & jnp.int32(-65536), jnp.float32)


def _gather_rows(idx_ref, tab_ref, tile_ref, t):
    for p in range(PEER_PAIRS):
        row = pl.multiple_of(idx_ref[p, t], ROW_WORDS)
        tile_ref[pl.ds(p, ROW_WORDS, stride=PLANE_STRIDE), :] = tab_ref[pl.ds(row, ROW_WORDS), :]


def _peer_u_kernel(idx_ref, x_ref, wt_ref, tab_ref, ct_ref, tile_ref):
    lane = lax.broadcasted_iota(jnp.int32, (PEER_PAIRS, PEER_TOKEN_TILE), 1)

    def token(t, act_t):
        _gather_rows(idx_ref, tab_ref, tile_ref, t)
        x = x_ref[t]
        acc = jnp.zeros((PEER_PAIRS, LANES), jnp.float32)
        for s in range(ROW_WORDS):
            w = tile_ref[pl.ds(s * PLANE_STRIDE, PEER_PAIRS), :]
            acc = acc + _unpack_lo(w) * x[s:s + 1, :] + _unpack_hi(w) * x[s + ROW_WORDS:s + ROW_WORDS + 1, :]
        a = jnp.sum(acc, axis=1, keepdims=True)
        return jnp.where(lane == t, a, act_t)

    act_t = lax.fori_loop(0, PEER_TOKEN_TILE, token, jnp.zeros((PEER_PAIRS, PEER_TOKEN_TILE), jnp.float32))
    gelu = 0.5 * act_t * (1.0 + lax.erf(act_t * (2.0 ** -0.5)))
    ct_ref[...] = gelu * wt_ref[...]


def _peer_v_kernel(idx_ref, ct_ref, tab_ref, out_ref, tile_ref):
    lane = lax.broadcasted_iota(jnp.int32, (PEER_PAIRS, PEER_TOKEN_TILE), 1)

    def token(t, carry):
        _gather_rows(idx_ref, tab_ref, tile_ref, t)
        c = jnp.sum(jnp.where(lane == t, ct_ref[...], 0.0), axis=1, keepdims=True)
        lo, hi = [], []
        for s in range(ROW_WORDS):
            w = tile_ref[pl.ds(s * PLANE_STRIDE, PEER_PAIRS), :]
            lo.append(jnp.sum(_unpack_lo(w) * c, axis=0, keepdims=True))
            hi.append(jnp.sum(_unpack_hi(w) * c, axis=0, keepdims=True))
        out_ref[t] = jnp.concatenate(lo + hi, axis=0)
        return carry

    lax.fori_loop(0, PEER_TOKEN_TILE, token, 0)


def _resident_spec(shape):
    return pl.BlockSpec(shape, lambda i: (0,) * len(shape), pipeline_mode=pl.Buffered(1))


def _peer_retrieve(x, idx_t, wts_t, u_packed, v_packed):
    n = x.shape[0]
    grid = (n // PEER_TOKEN_TILE,)
    params = pltpu.CompilerParams(dimension_semantics=("arbitrary",), vmem_limit_bytes=VMEM_LIMIT_BYTES)
    idx_spec = pl.BlockSpec((PEER_PAIRS, PEER_TOKEN_TILE), lambda i: (0, i), memory_space=pltpu.SMEM)
    pair_spec = pl.BlockSpec((PEER_PAIRS, PEER_TOKEN_TILE), lambda i: (0, i))
    row_spec = pl.BlockSpec((PEER_TOKEN_TILE, SUBLANES, LANES), lambda i: (i, 0, 0))
    tile = pltpu.VMEM((PLANE_ROWS, LANES), jnp.int32)
    ct = pl.pallas_call(
        _peer_u_kernel,
        grid=grid,
        in_specs=[idx_spec, row_spec, pair_spec, _resident_spec(u_packed.shape)],
        out_specs=pair_spec,
        out_shape=jax.ShapeDtypeStruct((PEER_PAIRS, n), jnp.float32),
        scratch_shapes=[tile],
        compiler_params=params,
        name="peer_u",
    )(idx_t, x.reshape(n, SUBLANES, LANES), wts_t, u_packed)
    out = pl.pallas_call(
        _peer_v_kernel,
        grid=grid,
        in_specs=[idx_spec, pair_spec, _resident_spec(v_packed.shape)],
        out_specs=row_spec,
        out_shape=jax.ShapeDtypeStruct((n, SUBLANES, LANES), jnp.float32),
        scratch_shapes=[tile],
        compiler_params=params,
        name="peer_v",
    )(idx_t, ct, v_packed)
    return out.reshape(n, D_MODEL)


SEL_TILE = 256
SEL_CHUNK = LANES
_BIG_POS = 1 << 20


def _allreduce8(x, op):
    for shift in (4, 2, 1):
        x = op(x, pltpu.roll(x, shift, 0))
    return x


def _top16(s3, pos3, payload3=None):
    vals, sels, pays = [], [], []
    for _ in range(PEER_TOPK):
        m = _allreduce8(jnp.max(s3, axis=0), jnp.maximum)
        sel = _allreduce8(jnp.min(jnp.where(s3 == m[None], pos3, _BIG_POS), axis=0), jnp.minimum)
        hit = pos3 == sel[None]
        if payload3 is not None:
            pays.append(_allreduce8(jnp.max(jnp.where(hit, payload3, -1), axis=0), jnp.maximum))
        s3 = jnp.where(hit, -jnp.inf, s3)
        vals.append(m)
        sels.append(sel)
    return vals, sels, pays


def _rows_to_tiles(rows, sub):
    tiles = []
    for half in range(2):
        t = rows[half * SUBLANES]
        for k in range(1, SUBLANES):
            t = jnp.where(sub == k, rows[half * SUBLANES + k], t)
        tiles.append(t)
    return tiles


def _peer_select_kernel(h_ref, wqt_ref, k1_ref, k2_ref, idx_ref, wts_ref, qt_ref):
    bf16 = jnp.bfloat16
    qt_ref[...] = lax.dot_general(wqt_ref[...], h_ref[...].astype(bf16), (((1,), (1,)), ((), ())),
                                  preferred_element_type=jnp.float32)
    n_chunks = SEL_TILE // SEL_CHUNK
    g1 = PEER_KEYS // SUBLANES
    key_pos = (lax.broadcasted_iota(jnp.int32, (g1, SUBLANES, SEL_CHUNK), 0) * SUBLANES
               + lax.broadcasted_iota(jnp.int32, (g1, SUBLANES, SEL_CHUNK), 1))
    g2 = PEER_TOPK * PEER_TOPK // SUBLANES
    cand_pos = (lax.broadcasted_iota(jnp.int32, (g2, SUBLANES, SEL_CHUNK), 0) * SUBLANES
                + lax.broadcasted_iota(jnp.int32, (g2, SUBLANES, SEL_CHUNK), 1))
    sub = lax.broadcasted_iota(jnp.int32, (SUBLANES, SEL_CHUNK), 0)

    def head_chunk(j, carry):
        hd = j // n_chunks
        lanes = pl.ds(pl.multiple_of((j % n_chunks) * SEL_CHUNK, SEL_CHUNK), SEL_CHUNK)
        q_row = pl.multiple_of(hd * PEER_QUERY_DIM, PEER_QUERY_DIM)
        q1 = qt_ref[pl.ds(q_row, PEER_HALF), lanes]
        q2 = qt_ref[pl.ds(q_row + PEER_HALF, PEER_HALF), lanes]
        s1 = jnp.dot(k1_ref[hd], q1.astype(bf16), preferred_element_type=jnp.float32)
        s2 = jnp.dot(k2_ref[hd], q2.astype(bf16), preferred_element_type=jnp.float32)
        v1, i1, _ = _top16(s1.reshape(g1, SUBLANES, SEL_CHUNK), key_pos)
        v2, i2, _ = _top16(s2.reshape(g1, SUBLANES, SEL_CHUNK), key_pos)
        v2t = _rows_to_tiles(v2, sub)
        i2t = _rows_to_tiles(i2, sub)
        cand = jnp.stack([v1[a] + v2t[bh] for a in range(PEER_TOPK) for bh in range(2)])
        cand_e = jnp.stack([i1[a] * PEER_KEYS + i2t[bh] for a in range(PEER_TOPK) for bh in range(2)])
        top_s, _, top_e = _top16(cand, cand_pos, cand_e)
        ts = _rows_to_tiles(top_s, sub)
        te = _rows_to_tiles(top_e, sub)
        e0 = jnp.exp(ts[0] - top_s[0])
        e1 = jnp.exp(ts[1] - top_s[0])
        denom = _allreduce8(e0 + e1, jnp.add)
        row = pl.multiple_of(hd * PEER_TOPK, PEER_TOPK)
        idx_ref[pl.ds(row, SUBLANES), lanes] = te[0] * ROW_WORDS
        idx_ref[pl.ds(row + SUBLANES, SUBLANES), lanes] = te[1] * ROW_WORDS
        wts_ref[pl.ds(row, SUBLANES), lanes] = e0 / denom
        wts_ref[pl.ds(row + SUBLANES, SUBLANES), lanes] = e1 / denom
        return carry

    lax.fori_loop(0, PEER_HEADS * n_chunks, head_chunk, 0)


def _const_spec(shape):
    return pl.BlockSpec(shape, lambda i: (0,) * len(shape))


def _peer_select(h2, w_q, sub_k1, sub_k2):
    n = h2.shape[0]
    bf16 = jnp.bfloat16
    pair_spec = pl.BlockSpec((PEER_PAIRS, SEL_TILE), lambda i: (0, i))
    return pl.pallas_call(
        _peer_select_kernel,
        grid=(n // SEL_TILE,),
        in_specs=[pl.BlockSpec((SEL_TILE, D_MODEL), lambda i: (i, 0)),
                  _const_spec((PEER_HEADS * PEER_QUERY_DIM, D_MODEL)),
                  _const_spec(sub_k1.shape), _const_spec(sub_k2.shape)],
        out_specs=[pair_spec, pair_spec],
        out_shape=[jax.ShapeDtypeStruct((PEER_PAIRS, n), jnp.int32), jax.ShapeDtypeStruct((PEER_PAIRS, n), jnp.float32)],
        scratch_shapes=[pltpu.VMEM((PEER_HEADS * PEER_QUERY_DIM, SEL_TILE), jnp.float32)],
        compiler_params=pltpu.CompilerParams(dimension_semantics=("arbitrary",), vmem_limit_bytes=VMEM_LIMIT_BYTES),
        name="peer_select",
    )(h2, w_q.T.astype(bf16), sub_k1.astype(bf16), sub_k2.astype(bf16))


ROW_TILE = 512
SMALL_WIDTH = LANES
PROJ_GROUPS = (("fox_qkv", 3 * FOX_WIDTH), ("pool_in", POOL_WIDTH), ("gdn_qkv", 3 * GDN_WIDTH),
               ("gdn_z", GDN_WIDTH), ("gates", N_BRANCHES * D_MODEL), ("small", SMALL_WIDTH))


PROJ_DTYPES = {"fox_qkv": jnp.bfloat16}


def _dense_params():
    return pltpu.CompilerParams(dimension_semantics=("arbitrary",), vmem_limit_bytes=VMEM_LIMIT_BYTES)


def _rearrange_w_in(w_in):
    offs = np.concatenate([[0], np.cumsum(IN_SPLITS)])
    fox_qkv, fox_f, pool_in, gdn_qkv, gdn_b, gdn_a, gdn_z, gates = [w_in[:, int(offs[i]):int(offs[i + 1])] for i in range(8)]
    small = jnp.concatenate([fox_f, gdn_b, gdn_a], axis=1)
    small = jnp.pad(small, ((0, 0), (0, SMALL_WIDTH - small.shape[1])))
    return jnp.concatenate([fox_qkv, pool_in, gdn_qkv, gdn_z, gates, small], axis=1).astype(jnp.bfloat16)


def _inproj_kernel(h_ref, w_ref, *out_refs):
    x = h_ref[...].astype(jnp.bfloat16)
    off = 0
    for (_, width), o_ref in zip(PROJ_GROUPS, out_refs):
        o_ref[...] = jnp.dot(x, w_ref[:, off:off + width], preferred_element_type=jnp.float32).astype(o_ref.dtype)
        off += width


def _inproj(h2, w_r):
    n = h2.shape[0]
    outs = pl.pallas_call(
        _inproj_kernel,
        grid=(n // ROW_TILE,),
        in_specs=[pl.BlockSpec((ROW_TILE, D_MODEL), lambda i: (i, 0)), _const_spec(w_r.shape)],
        out_specs=[pl.BlockSpec((ROW_TILE, w), lambda i: (i, 0)) for _, w in PROJ_GROUPS],
        out_shape=[jax.ShapeDtypeStruct((n, w), PROJ_DTYPES.get(name, jnp.float32)) for name, w in PROJ_GROUPS],
        compiler_params=_dense_params(),
        name="inproj",
    )(h2, w_r)
    return {name: o for (name, _), o in zip(PROJ_GROUPS, outs)}


def _layer_norm_rows(y, g, b):
    mu = jnp.mean(y, axis=-1, keepdims=True)
    d = y - mu
    var = jnp.mean(d * d, axis=-1, keepdims=True)
    return d * lax.rsqrt(var + LN_EPS) * g + b


def _merge_kernel(h_ref, a_ref, b_ref, c_ref, gates_ref, wb_ref, wo_ref, g_ref, beta_ref, o_ref):
    bf16 = jnp.bfloat16
    merged = jnp.zeros((ROW_TILE, D_MODEL), jnp.float32)
    for n, br_ref in enumerate((a_ref, b_ref, c_ref)):
        to_model = jnp.dot(br_ref[...].astype(bf16), wb_ref[n], preferred_element_type=jnp.float32)
        merged = merged + jax.nn.sigmoid(gates_ref[:, n * D_MODEL:(n + 1) * D_MODEL]) * to_model
    mix = jnp.dot(merged.astype(bf16), wo_ref[...], preferred_element_type=jnp.float32)
    o_ref[...] = _layer_norm_rows(DEEPNORM_ALPHA * h_ref[...] + mix, g_ref[...], beta_ref[...])


def _merge(h2, br_a, br_b, br_c, gates, w_branch, w_out, ln_g, ln_b):
    n = h2.shape[0]
    bf16 = jnp.bfloat16
    row = lambda w: pl.BlockSpec((ROW_TILE, w), lambda i: (i, 0))
    return pl.pallas_call(
        _merge_kernel,
        grid=(n // ROW_TILE,),
        in_specs=[row(D_MODEL), row(FOX_WIDTH), row(POOL_WIDTH), row(GDN_WIDTH), row(N_BRANCHES * D_MODEL),
                  _const_spec(w_branch.shape), _const_spec(w_out.shape), _const_spec((1, D_MODEL)), _const_spec((1, D_MODEL))],
        out_specs=row(D_MODEL),
        out_shape=jax.ShapeDtypeStruct((n, D_MODEL), jnp.float32),
        compiler_params=_dense_params(),
        name="merge_ln1",
    )(h2, br_a, br_b, br_c, gates, w_branch.astype(bf16), w_out.astype(bf16), ln_g.reshape(1, -1), ln_b.reshape(1, -1))


def _ple_kernel(h_ref, p_ref, ffn_ref, wg_ref, wp_ref, g_ref, beta_ref, o_ref):
    bf16 = jnp.bfloat16
    h = h_ref[...]
    gate = jax.nn.sigmoid(jnp.dot(h.astype(bf16), wg_ref[...], preferred_element_type=jnp.float32))
    emb = jnp.dot(p_ref[...].astype(bf16), wp_ref[...], preferred_element_type=jnp.float32)
    o_ref[...] = _layer_norm_rows(DEEPNORM_ALPHA * h + ffn_ref[...] + gate * emb, g_ref[...], beta_ref[...])


def _ple_ln(h2, p2, ffn, w_g, w_p, ln_g, ln_b):
    n = h2.shape[0]
    bf16 = jnp.bfloat16
    row = lambda w: pl.BlockSpec((ROW_TILE, w), lambda i: (i, 0))
    return pl.pallas_call(
        _ple_kernel,
        grid=(n // ROW_TILE,),
        in_specs=[row(D_MODEL), row(p2.shape[1]), row(D_MODEL), _const_spec(w_g.shape), _const_spec(w_p.shape),
                  _const_spec((1, D_MODEL)), _const_spec((1, D_MODEL))],
        out_specs=row(D_MODEL),
        out_shape=jax.ShapeDtypeStruct((n, D_MODEL), jnp.float32),
        compiler_params=_dense_params(),
        name="ple_ln2",
    )(h2, p2, ffn, w_g.astype(bf16), w_p.astype(bf16), ln_g.reshape(1, -1), ln_b.reshape(1, -1))


SEQ_TILE = 512
LANE_FOX = 0
LANE_BETA = FOX_HEADS
LANE_DECAY = FOX_HEADS + GDN_HEADS


def _split3_dot(tri, x):
    bf16 = jnp.bfloat16
    hi = x.astype(bf16)
    r1 = x - hi.astype(jnp.float32)
    mid = r1.astype(bf16)
    lo = (r1 - mid.astype(jnp.float32)).astype(bf16)
    dot = lambda t: jnp.dot(tri, t, preferred_element_type=jnp.float32)
    return dot(hi) + dot(mid) + dot(lo)


def _softplus(x):
    return jnp.maximum(x, 0.0) + jnp.log(1.0 + jnp.exp(-jnp.abs(x)))


def _gate_prep_kernel(small_ref, bias_ref, alog_ref, aux_ref, cumt_ref, carry_ref):
    @pl.when(pl.program_id(1) == 0)
    def _():
        carry_ref[...] = jnp.zeros_like(carry_ref)

    x = small_ref[0] + bias_ref[...]
    lane = lax.broadcasted_iota(jnp.int32, x.shape, 1)
    is_fox = lane < LANE_BETA
    is_beta = (lane >= LANE_BETA) & (lane < LANE_DECAY)
    log_f = -_softplus(-x)
    decay = -jnp.exp(alog_ref[...]) * _softplus(x)
    r = lax.broadcasted_iota(jnp.int32, (SEQ_TILE, SEQ_TILE), 0)
    c = lax.broadcasted_iota(jnp.int32, (SEQ_TILE, SEQ_TILE), 1)
    tri = (c <= r).astype(jnp.bfloat16)
    chunk_shift = GDN_CHUNK.bit_length() - 1
    tri_chunk = ((c <= r) & ((r >> chunk_shift) == (c >> chunk_shift))).astype(jnp.bfloat16)
    cum = _split3_dot(tri, jnp.where(is_fox, log_f, 0.0)) + carry_ref[...]
    carry_ref[...] = cum[SEQ_TILE - 1:SEQ_TILE, :]
    gcum = _split3_dot(tri_chunk, jnp.where(is_fox | is_beta, 0.0, decay))
    aux_ref[0] = jnp.where(is_fox, cum, jnp.where(is_beta, jax.nn.sigmoid(x), gcum))
    cumt_ref[0] = cum.T[:SUBLANES, :]


def _gate_prep(small, B, S, fox_fb, gdn_a_log, gdn_dt_bias):
    pad = lambda v, at: jnp.zeros((1, SMALL_WIDTH), jnp.float32).at[0, at:at + v.shape[0]].set(v)
    bias = pad(fox_fb, LANE_FOX) + pad(gdn_dt_bias, LANE_DECAY)
    alog = pad(gdn_a_log, LANE_DECAY)
    return pl.pallas_call(
        _gate_prep_kernel,
        grid=(B, S // SEQ_TILE),
        in_specs=[pl.BlockSpec((1, SEQ_TILE, SMALL_WIDTH), lambda b, s: (b, s, 0)),
                  pl.BlockSpec((1, SMALL_WIDTH), lambda b, s: (0, 0)), pl.BlockSpec((1, SMALL_WIDTH), lambda b, s: (0, 0))],
        out_specs=[pl.BlockSpec((1, SEQ_TILE, SMALL_WIDTH), lambda b, s: (b, s, 0)),
                   pl.BlockSpec((1, SUBLANES, SEQ_TILE), lambda b, s: (b, 0, s))],
        out_shape=[jax.ShapeDtypeStruct((B, S, SMALL_WIDTH), jnp.float32), jax.ShapeDtypeStruct((B, SUBLANES, S), jnp.float32)],
        scratch_shapes=[pltpu.VMEM((1, SMALL_WIDTH), jnp.float32)],
        compiler_params=pltpu.CompilerParams(dimension_semantics=("arbitrary", "arbitrary"), vmem_limit_bytes=VMEM_LIMIT_BYTES),
        name="gate_prep",
    )(small.reshape(B, S, SMALL_WIDTH), bias, alog)


ATT_TILE = 256
HEADS_PER_BLOCK = LANES // FOX_HEAD_DIM


def _fox_kernel(q_ref, k_ref, v_ref, cum_ref, cumt_ref, o_ref):
    bf16 = jnp.bfloat16
    hp = pl.program_id(1)
    qi = pl.program_id(2)
    q = q_ref[0] * jnp.asarray(FOX_HEAD_DIM ** -0.5, bf16)
    lane = lax.broadcasted_iota(jnp.int32, (ATT_TILE, LANES), 1)
    cum_q = cum_ref[0]
    lane_q = lax.broadcasted_iota(jnp.int32, cum_q.shape, 1)
    sub_k = lax.broadcasted_iota(jnp.int32, (SUBLANES, ATT_TILE), 0)
    diag = (lax.broadcasted_iota(jnp.int32, (ATT_TILE, ATT_TILE), 0)
            >= lax.broadcasted_iota(jnp.int32, (ATT_TILE, ATT_TILE), 1))
    outs = []
    for a in range(HEADS_PER_BLOCK):
        head = hp * HEADS_PER_BLOCK + a
        in_head = (lane >= a * FOX_HEAD_DIM) & (lane < (a + 1) * FOX_HEAD_DIM)
        qa = jnp.where(in_head, q, jnp.zeros_like(q))
        cq = jnp.sum(jnp.where(lane_q == head, cum_q, 0.0), axis=1, keepdims=True)

        def block(j, carry, masked):
            m, l, acc = carry
            rows = pl.ds(pl.multiple_of(j * ATT_TILE, ATT_TILE), ATT_TILE)
            s = lax.dot_general(qa, k_ref[0, rows, :], (((1,), (1,)), ((), ())), preferred_element_type=jnp.float32)
            ck = jnp.sum(jnp.where(sub_k == head, cumt_ref[0, :, rows], 0.0), axis=0, keepdims=True)
            s = s + cq - ck
            if masked:
                s = jnp.where(diag, s, -jnp.inf)
            m_new = jnp.maximum(m, jnp.max(s, axis=1, keepdims=True))
            alpha = jnp.exp(m - m_new)
            p = jnp.exp(s - m_new)
            l = alpha * l + jnp.sum(p, axis=1, keepdims=True)
            acc = alpha * acc + jnp.dot(p.astype(bf16), v_ref[0, rows, :], preferred_element_type=jnp.float32)
            return m_new, l, acc

        init = (jnp.full((ATT_TILE, 1), -jnp.inf, jnp.float32), jnp.zeros((ATT_TILE, 1), jnp.float32),
                jnp.zeros((ATT_TILE, LANES), jnp.float32))
        carry = lax.fori_loop(0, qi, functools.partial(block, masked=False), init)
        _, l, acc = block(qi, carry, masked=True)
        outs.append((in_head, acc / l))
    out = outs[0][1]
    for in_head, val in outs[1:]:
        out = jnp.where(in_head, val, out)
    o_ref[0] = out


def _fox_attention(fox_qkv, aux, cumt, B, S):
    qkv = fox_qkv.reshape(B, S, 3 * FOX_WIDTH)
    blocks = FOX_WIDTH // LANES
    out = pl.pallas_call(
        _fox_kernel,
        grid=(B, blocks, S // ATT_TILE),
        in_specs=[pl.BlockSpec((1, ATT_TILE, LANES), lambda b, hp, qi: (b, qi, hp)),
                  pl.BlockSpec((1, S, LANES), lambda b, hp, qi: (b, 0, blocks + hp)),
                  pl.BlockSpec((1, S, LANES), lambda b, hp, qi: (b, 0, 2 * blocks + hp)),
                  pl.BlockSpec((1, ATT_TILE, SMALL_WIDTH), lambda b, hp, qi: (b, qi, 0)),
                  pl.BlockSpec((1, SUBLANES, S), lambda b, hp, qi: (b, 0, 0))],
        out_specs=pl.BlockSpec((1, ATT_TILE, LANES), lambda b, hp, qi: (b, qi, hp)),
        out_shape=jax.ShapeDtypeStruct((B, S, FOX_WIDTH), jnp.float32),
        compiler_params=pltpu.CompilerParams(dimension_semantics=("arbitrary",) * 3, vmem_limit_bytes=VMEM_LIMIT_BYTES),
        name="fox_attention",
    )(qkv, qkv, qkv, aux, cumt)
    return out.reshape(B * S, FOX_WIDTH)


POOL_HALO = SUBLANES * 2


def _shift_rows(x, k):
    return pltpu.roll(x, k, 0)


def _pool_kernel(x_ref, w_ref, scale_ref, o_ref, halo_ref):
    @pl.when(pl.program_id(1) == 0)
    def _():
        halo_ref[...] = jnp.zeros_like(halo_ref)

    s_blk = pl.program_id(1)
    rows = POOL_HALO + SEQ_TILE
    row = lax.broadcasted_iota(jnp.int32, (rows, POOL_GROUP_DIM), 0)
    t_abs = s_blk * SEQ_TILE + row - POOL_HALO
    for g, window in enumerate(POOL_WINDOWS):
        cols = slice(g * POOL_GROUP_DIM, (g + 1) * POOL_GROUP_DIM)
        x = x_ref[0, :, cols]
        ext = jnp.concatenate([halo_ref[:, cols], x], axis=0)
        acc = ext
        span = 1
        while span < window:
            acc = acc + jnp.where(row >= span, _shift_rows(acc, span), 0.0)
            span *= 2
        win_sum = acc[POOL_HALO:, :]
        count = jnp.minimum(t_abs[POOL_HALO:, :] + 1, window).astype(jnp.float32)
        pooled = win_sum / count - x
        mixed = jnp.dot(pooled.astype(jnp.bfloat16), w_ref[g], preferred_element_type=jnp.float32)
        o_ref[0, :, cols] = mixed * scale_ref[:, cols]
    halo_ref[...] = x_ref[0, SEQ_TILE - POOL_HALO:, :]


def _multiscale_pool(pool_in, w_pool, pool_scale, B, S):
    out = pl.pallas_call(
        _pool_kernel,
        grid=(B, S // SEQ_TILE),
        in_specs=[pl.BlockSpec((1, SEQ_TILE, POOL_WIDTH), lambda b, s: (b, s, 0)),
                  pl.BlockSpec(w_pool.shape, lambda b, s: (0, 0, 0)), pl.BlockSpec((1, POOL_WIDTH), lambda b, s: (0, 0))],
        out_specs=pl.BlockSpec((1, SEQ_TILE, POOL_WIDTH), lambda b, s: (b, s, 0)),
        out_shape=jax.ShapeDtypeStruct((B, S, POOL_WIDTH), jnp.float32),
        scratch_shapes=[pltpu.VMEM((POOL_HALO, POOL_WIDTH), jnp.float32)],
        compiler_params=pltpu.CompilerParams(dimension_semantics=("arbitrary", "arbitrary"), vmem_limit_bytes=VMEM_LIMIT_BYTES),
        name="pool",
    )(pool_in.reshape(B, S, POOL_WIDTH), w_pool.astype(jnp.bfloat16), pool_scale.reshape(1, POOL_WIDTH))
    return out.reshape(B * S, POOL_WIDTH)


GDN_TILE_CHUNKS = SEQ_TILE // GDN_CHUNK
GDN_HALO = SUBLANES


def _l2norm(x):
    return x * lax.rsqrt(jnp.sum(x * x, axis=-1, keepdims=True) + L2_EPS)


def _silu(x):
    return x * jax.nn.sigmoid(x)


def _bdot(a, b, dims):
    return lax.dot_general(a.astype(jnp.bfloat16), b.astype(jnp.bfloat16), ((dims[0], dims[1]), ((0,), (0,))),
                           preferred_element_type=jnp.float32)


def _bdot3(a, b):
    bf16 = jnp.bfloat16
    a_hi, b_hi = a.astype(bf16), b.astype(bf16)
    a_lo = (a - a_hi.astype(jnp.float32)).astype(bf16)
    b_lo = (b - b_hi.astype(jnp.float32)).astype(bf16)
    dot = lambda x, y: lax.dot_general(x, y, (((2,), (1,)), ((0,), (0,))), preferred_element_type=jnp.float32)
    return dot(a_hi, b_hi) + dot(a_hi, b_lo) + dot(a_lo, b_hi)


def _unit_lower_inverse(low):
    c = low.shape[-1]
    eye = (lax.broadcasted_iota(jnp.int32, (c, c), 0) == lax.broadcasted_iota(jnp.int32, (c, c), 1)).astype(jnp.float32)
    inv = eye[None] - low
    power = low
    span = 2
    while span < c:
        power = _bdot3(power, power)
        inv = inv + _bdot3(inv, power)
        span *= 2
    return inv


def _gdn_kernel(qkv_ref, z_ref, aux_ref, convw_ref, norm_ref, o_ref, halo_ref, state_ref):
    @pl.when(pl.program_id(1) == 0)
    def _():
        halo_ref[...] = jnp.zeros_like(halo_ref)
        state_ref[...] = jnp.zeros_like(state_ref)

    nc, c, d = GDN_TILE_CHUNKS, GDN_CHUNK, GDN_HEAD_DIM
    x = qkv_ref[0]
    ext = jnp.concatenate([halo_ref[...], x], axis=0)
    conv = ext * convw_ref[GDN_CONV - 1:GDN_CONV, :]
    for k in range(1, GDN_CONV):
        conv = conv + pltpu.roll(ext, k, 0) * convw_ref[GDN_CONV - 1 - k:GDN_CONV - k, :]
    act = _silu(conv[GDN_HALO:, :])
    halo_ref[...] = x[SEQ_TILE - GDN_HALO:, :]

    aux = aux_ref[0]
    aux_t = aux.T
    lane = lax.broadcasted_iota(jnp.int32, aux.shape, 1)
    ri = lax.broadcasted_iota(jnp.int32, (c, c), 0)
    ci = lax.broadcasted_iota(jnp.int32, (c, c), 1)
    for hd in range(GDN_HEADS):
        cols = slice(hd * d, (hd + 1) * d)
        q = _l2norm(act[:, cols]) * (d ** -0.5)
        k = _l2norm(act[:, GDN_WIDTH + hd * d:GDN_WIDTH + (hd + 1) * d])
        v = act[:, 2 * GDN_WIDTH + hd * d:2 * GDN_WIDTH + (hd + 1) * d]
        beta = jnp.sum(jnp.where(lane == LANE_BETA + hd, aux, 0.0), axis=1, keepdims=True)
        g_col = jnp.sum(jnp.where(lane == LANE_DECAY + hd, aux, 0.0), axis=1, keepdims=True)
        g_row = aux_t[LANE_DECAY + hd:LANE_DECAY + hd + 1, :]
        diff = jnp.stack([g_col[i * c:(i + 1) * c, :] - g_row[:, i * c:(i + 1) * c] for i in range(nc)])
        decay_strict = jnp.exp(jnp.where((ri > ci)[None], diff, -jnp.inf))
        decay_incl = jnp.exp(jnp.where((ri >= ci)[None], diff, -jnp.inf))
        k_beta = k * beta
        q3, k3 = q.reshape(nc, c, d), k.reshape(nc, c, d)
        low = _bdot(k_beta.reshape(nc, c, d), k3, ((2,), (2,))) * decay_strict
        t_inv = _unit_lower_inverse(low)
        u = _bdot(t_inv, (v * beta).reshape(nc, c, d), ((2,), (1,)))
        w = _bdot(t_inv, (k_beta * jnp.exp(g_col)).reshape(nc, c, d), ((2,), (1,)))
        a_intra = _bdot(q3, k3, ((2,), (2,))) * decay_incl
        q_dec = (q * jnp.exp(g_col)).reshape(nc, c, d)
        g3 = g_col.reshape(nc, c, 1)
        g_last = g3[:, c - 1:c, :]
        k_dec = k3 * jnp.exp(g_last - g3)
        state = state_ref[hd]
        outs = []
        for i in range(nc):
            bf16 = jnp.bfloat16
            state_b = state.astype(bf16)
            v_new = u[i] - jnp.dot(w[i].astype(bf16), state_b, preferred_element_type=jnp.float32)
            v_new_b = v_new.astype(bf16)
            outs.append(jnp.dot(q_dec[i].astype(bf16), state_b, preferred_element_type=jnp.float32)
                        + jnp.dot(a_intra[i].astype(bf16), v_new_b, preferred_element_type=jnp.float32))
            state = state * jnp.exp(g_last[i]) + lax.dot_general(
                k_dec[i].astype(bf16), v_new_b, (((0,), (0,)), ((), ())), preferred_element_type=jnp.float32)
        state_ref[hd] = state
        o = jnp.concatenate(outs, axis=0)
        o = o * lax.rsqrt(jnp.mean(o * o, axis=-1, keepdims=True) + RMS_EPS) * norm_ref[...]
        o_ref[0, :, cols] = o * _silu(z_ref[0, :, cols])


def _gated_deltanet(gdn_qkv, gdn_z, aux, conv_w, norm_g, B, S):
    width = 3 * GDN_WIDTH
    out = pl.pallas_call(
        _gdn_kernel,
        grid=(B, S // SEQ_TILE),
        in_specs=[pl.BlockSpec((1, SEQ_TILE, width), lambda b, s: (b, s, 0)),
                  pl.BlockSpec((1, SEQ_TILE, GDN_WIDTH), lambda b, s: (b, s, 0)),
                  pl.BlockSpec((1, SEQ_TILE, SMALL_WIDTH), lambda b, s: (b, s, 0)),
                  pl.BlockSpec((GDN_CONV, width), lambda b, s: (0, 0)),
                  pl.BlockSpec((1, GDN_HEAD_DIM), lambda b, s: (0, 0))],
        out_specs=pl.BlockSpec((1, SEQ_TILE, GDN_WIDTH), lambda b, s: (b, s, 0)),
        out_shape=jax.ShapeDtypeStruct((B, S, GDN_WIDTH), jnp.float32),
        scratch_shapes=[pltpu.VMEM((GDN_HALO, width), jnp.float32),
                        pltpu.VMEM((GDN_HEADS, GDN_HEAD_DIM, GDN_HEAD_DIM), jnp.float32)],
        compiler_params=pltpu.CompilerParams(dimension_semantics=("arbitrary", "arbitrary"), vmem_limit_bytes=VMEM_LIMIT_BYTES),
        name="gated_deltanet",
    )(gdn_qkv.reshape(B, S, width), gdn_z.reshape(B, S, GDN_WIDTH), aux, conv_w.reshape(GDN_CONV, width),
      norm_g.reshape(1, GDN_HEAD_DIM))
    return out.reshape(B * S, GDN_WIDTH)


def _mixer_branches(proj, B, S, fox_fb, pool_w, pool_scale, gdn_conv, gdn_a_log, gdn_dt_bias, gdn_norm):
    aux, cumt = _gate_prep(proj["small"], B, S, fox_fb, gdn_a_log, gdn_dt_bias)
    branch_a = _fox_attention(proj["fox_qkv"], aux, cumt, B, S)
    branch_b = _multiscale_pool(proj["pool_in"], pool_w, pool_scale, B, S)
    branch_c = _gated_deltanet(proj["gdn_qkv"], proj["gdn_z"], aux, gdn_conv, gdn_norm, B, S)
    return branch_a, branch_b, branch_c


def kernel(x, p, w_in, fox_fb, pool_w, pool_scale, gdn_conv, gdn_a_log, gdn_dt_bias, gdn_norm, w_branch, w_out, ln1_g, ln1_b, peer_wq, peer_k1, peer_k2, peer_u, peer_v, ple_gate, ple_proj, ln2_g, ln2_b):
    B, S, D = x.shape
    h2 = x.reshape(B * S, D)
    for i in range(DEPTH):
        proj = _inproj(h2, _rearrange_w_in(w_in[i]))
        br_a, br_b, br_c = _mixer_branches(proj, B, S, fox_fb[i], pool_w[i], pool_scale[i], gdn_conv[i], gdn_a_log[i],
                                           gdn_dt_bias[i], gdn_norm[i])
        h2 = _merge(h2, br_a, br_b, br_c, proj["gates"], w_branch[i], w_out[i], ln1_g[i], ln1_b[i])
        idx_t, wts_t = _peer_select(h2, peer_wq[i], peer_k1[i], peer_k2[i])
        ffn = _peer_retrieve(h2, idx_t, wts_t, _pack_table(peer_u[i]), _pack_table(peer_v[i]))
        h2 = _ple_ln(h2, p[i].reshape(B * S, -1), ffn, ple_gate[i], ple_proj[i], ln2_g[i], ln2_b[i])
    return h2.reshape(B, S, D)
```

```python
import functools
import math

import jax
import jax.numpy as jnp
import numpy as np
from jax import lax
from jax.experimental import pallas as pl
from jax.experimental.pallas import tpu as pltpu

D_MODEL = 1024
DEPTH = 2
FOX_HEADS = 8
FOX_HEAD_DIM = 64
FOX_WIDTH = FOX_HEADS * FOX_HEAD_DIM
Q_BLOCK = 128
POOL_WINDOWS = (2, 4, 8, 16)
POOL_GROUPS = 4
POOL_GROUP_DIM = 128
POOL_WIDTH = POOL_GROUPS * POOL_GROUP_DIM
GDN_HEADS = 4
GDN_HEAD_DIM = 128
GDN_WIDTH = GDN_HEADS * GDN_HEAD_DIM
GDN_CONV = 4
GDN_CHUNK = 64
N_BRANCHES = 3
IN_SPLITS = (3 * FOX_WIDTH, FOX_HEADS, POOL_WIDTH, 3 * GDN_WIDTH, GDN_HEADS, GDN_HEADS, GDN_WIDTH, N_BRANCHES * D_MODEL)
PEER_HEADS = 8
PEER_KEYS = 128
PEER_N_EXPERTS = PEER_KEYS * PEER_KEYS
PEER_QUERY_DIM = 256
PEER_HALF = PEER_QUERY_DIM // 2
PEER_TOPK = 16
PEER_PAIRS = PEER_HEADS * PEER_TOPK
DEEPNORM_ALPHA = (2 * DEPTH) ** 0.25
LN_EPS = 1e-5
RMS_EPS = 1e-6
L2_EPS = 1e-6

LANES = 128
SUBLANES = 8
VMEM_LIMIT_BYTES = 52 * 1024 * 1024

PEER_TOKEN_TILE = 128
ROW_WORDS = D_MODEL // 2 // LANES
PLANE_STRIDE = PEER_PAIRS + 1
PLANE_ROWS = (ROW_WORDS - 1) * PLANE_STRIDE + PEER_PAIRS


def _pack_table(tab):
    e = tab.shape[0]
    b = lax.bitcast_convert_type(tab.astype(jnp.bfloat16), jnp.uint16).astype(jnp.uint32)
    b = b.reshape(e, 2, ROW_WORDS, LANES)
    w = b[:, 0] | (b[:, 1] << 16)
    return lax.bitcast_convert_type(w, jnp.int32).reshape(e * ROW_WORDS, LANES)


def _unpack_lo(w):
    return lax.bitcast_convert_type(w << 16, jnp.float32)


def _unpack_hi(w):
    return lax.bitcast_convert_type(w & jnp.int32(-65536), jnp.float32)


def _gather_rows(idx_ref, tab_ref, tile_ref, t):
    token_idx = idx_ref.at[t]
    for p in range(PEER_PAIRS):
        row = pl.multiple_of(token_idx[p], ROW_WORDS)
        tile_ref[pl.ds(p, ROW_WORDS, stride=PLANE_STRIDE), :] = tab_ref[pl.ds(row, ROW_WORDS), :]


def _token_pairs(idx_ref, tab_ref, tiles, compute, init):
    _gather_rows(idx_ref, tab_ref, tiles[0], 0)

    def pair(i, carry):
        t = 2 * i
        _gather_rows(idx_ref, tab_ref, tiles[1], t + 1)
        carry = compute(t, tiles[0], carry)
        _gather_rows(idx_ref, tab_ref, tiles[0], jnp.minimum(t + 2, PEER_TOKEN_TILE - 1))
        return compute(t + 1, tiles[1], carry)

    return lax.fori_loop(0, PEER_TOKEN_TILE // 2, pair, init)


def _peer_u_kernel(idx_ref, x_ref, wt_ref, tab_ref, ct_ref, tile0_ref, tile1_ref):
    lane = lax.broadcasted_iota(jnp.int32, (PEER_PAIRS, PEER_TOKEN_TILE), 1)

    def token(t, tile_ref, act_t):
        x = x_ref[t]
        acc = jnp.zeros((PEER_PAIRS, LANES), jnp.float32)
        for s in range(ROW_WORDS):
            w = tile_ref[pl.ds(s * PLANE_STRIDE, PEER_PAIRS), :]
            acc = acc + _unpack_lo(w) * x[s:s + 1, :] + _unpack_hi(w) * x[s + ROW_WORDS:s + ROW_WORDS + 1, :]
        a = jnp.sum(acc, axis=1, keepdims=True)
        return jnp.where(lane == t, a, act_t)

    act_t = _token_pairs(idx_ref, tab_ref, (tile0_ref, tile1_ref), token,
                         jnp.zeros((PEER_PAIRS, PEER_TOKEN_TILE), jnp.float32))
    gelu = 0.5 * act_t * (1.0 + lax.erf(act_t * (2.0 ** -0.5)))
    ct_ref[...] = gelu * wt_ref[...]


def _peer_v_kernel(idx_ref, ct_ref, tab_ref, out_ref, tile0_ref, tile1_ref):
    lane = lax.broadcasted_iota(jnp.int32, (PEER_PAIRS, PEER_TOKEN_TILE), 1)

    def token(t, tile_ref, carry):
        c = jnp.sum(jnp.where(lane == t, ct_ref[...], 0.0), axis=1, keepdims=True)
        lo, hi = [], []
        for s in range(ROW_WORDS):
            w = tile_ref[pl.ds(s * PLANE_STRIDE, PEER_PAIRS), :]
            lo.append(jnp.sum(_unpack_lo(w) * c, axis=0, keepdims=True))
            hi.append(jnp.sum(_unpack_hi(w) * c, axis=0, keepdims=True))
        out_ref[t] = jnp.concatenate(lo + hi, axis=0)
        return carry

    _token_pairs(idx_ref, tab_ref, (tile0_ref, tile1_ref), token, 0)


def _resident_spec(shape):
    return pl.BlockSpec(shape, lambda i: (0,) * len(shape), pipeline_mode=pl.Buffered(1))


def _peer_retrieve(x, idx, wts_t, u_packed, v_packed):
    n = x.shape[0]
    grid = (n // PEER_TOKEN_TILE,)
    params = pltpu.CompilerParams(dimension_semantics=("arbitrary",), vmem_limit_bytes=VMEM_LIMIT_BYTES)
    idx_spec = pl.BlockSpec((PEER_TOKEN_TILE, PEER_PAIRS), lambda i: (i, 0), memory_space=pltpu.SMEM)
    pair_spec = pl.BlockSpec((PEER_PAIRS, PEER_TOKEN_TILE), lambda i: (0, i))
    row_spec = pl.BlockSpec((PEER_TOKEN_TILE, SUBLANES, LANES), lambda i: (i, 0, 0))
    tiles = [pltpu.VMEM((PLANE_ROWS, LANES), jnp.int32)] * 2
    ct = pl.pallas_call(
        _peer_u_kernel,
        grid=grid,
        in_specs=[idx_spec, row_spec, pair_spec, _resident_spec(u_packed.shape)],
        out_specs=pair_spec,
        out_shape=jax.ShapeDtypeStruct((PEER_PAIRS, n), jnp.float32),
        scratch_shapes=tiles,
        compiler_params=params,
        name="peer_u",
    )(idx, x.reshape(n, SUBLANES, LANES), wts_t, u_packed)
    out = pl.pallas_call(
        _peer_v_kernel,
        grid=grid,
        in_specs=[idx_spec, pair_spec, _resident_spec(v_packed.shape)],
        out_specs=row_spec,
        out_shape=jax.ShapeDtypeStruct((n, SUBLANES, LANES), jnp.float32),
        scratch_shapes=tiles,
        compiler_params=params,
        name="peer_v",
    )(idx, ct, v_packed)
    return out.reshape(n, D_MODEL)


SEL_TILE = 256
SEL_CHUNK = LANES
_BIG_POS = 1 << 20
CAND_TILES = tuple((a, bh) for a in range(PEER_TOPK) for bh in range(2) if (a + 1) * (bh * SUBLANES + 1) <= PEER_TOPK)


def _allreduce8(x, op):
    for shift in (4, 2, 1):
        x = op(x, pltpu.roll(x, shift, 0))
    return x


def _top16(s3, pos3, payload3=None):
    vals, sels, pays = [], [], []
    for _ in range(PEER_TOPK):
        m = _allreduce8(jnp.max(s3, axis=0), jnp.maximum)
        sel = _allreduce8(jnp.min(jnp.where(s3 == m[None], pos3, _BIG_POS), axis=0), jnp.minimum)
        hit = pos3 == sel[None]
        if payload3 is not None:
            pays.append(_allreduce8(jnp.max(jnp.where(hit, payload3, -1), axis=0), jnp.maximum))
        s3 = jnp.where(hit, -jnp.inf, s3)
        vals.append(m)
        sels.append(sel)
    return vals, sels, pays


def _rows_to_tiles(rows, sub):
    tiles = []
    for half in range(2):
        t = rows[half * SUBLANES]
        for k in range(1, SUBLANES):
            t = jnp.where(sub == k, rows[half * SUBLANES + k], t)
        tiles.append(t)
    return tiles


def _peer_select_kernel(h_ref, wqt_ref, k1_ref, k2_ref, idx_ref, wts_ref, qt_ref, idxt_ref):
    bf16 = jnp.bfloat16
    qt_ref[...] = lax.dot_general(wqt_ref[...], h_ref[...].astype(bf16), (((1,), (1,)), ((), ())),
                                  preferred_element_type=jnp.float32)
    n_chunks = SEL_TILE // SEL_CHUNK
    g1 = PEER_KEYS // SUBLANES
    key_pos = (lax.broadcasted_iota(jnp.int32, (g1, SUBLANES, SEL_CHUNK), 0) * SUBLANES
               + lax.broadcasted_iota(jnp.int32, (g1, SUBLANES, SEL_CHUNK), 1))
    sub = lax.broadcasted_iota(jnp.int32, (SUBLANES, SEL_CHUNK), 0)
    cand_pos = jnp.stack([sub + (a * PEER_TOPK + bh * SUBLANES) for a, bh in CAND_TILES])

    def head_chunk(j, carry):
        hd = j // n_chunks
        lanes = pl.ds(pl.multiple_of((j % n_chunks) * SEL_CHUNK, SEL_CHUNK), SEL_CHUNK)
        q_row = pl.multiple_of(hd * PEER_QUERY_DIM, PEER_QUERY_DIM)
        q1 = qt_ref[pl.ds(q_row, PEER_HALF), lanes]
        q2 = qt_ref[pl.ds(q_row + PEER_HALF, PEER_HALF), lanes]
        s1 = jnp.dot(k1_ref[hd], q1.astype(bf16), preferred_element_type=jnp.float32)
        s2 = jnp.dot(k2_ref[hd], q2.astype(bf16), preferred_element_type=jnp.float32)
        v1, i1, _ = _top16(s1.reshape(g1, SUBLANES, SEL_CHUNK), key_pos)
        v2, i2, _ = _top16(s2.reshape(g1, SUBLANES, SEL_CHUNK), key_pos)
        v2t = _rows_to_tiles(v2, sub)
        i2t = _rows_to_tiles(i2, sub)
        cand = jnp.stack([v1[a] + v2t[bh] for a, bh in CAND_TILES])
        cand_e = jnp.stack([i1[a] * PEER_KEYS + i2t[bh] for a, bh in CAND_TILES])
        top_s, _, top_e = _top16(cand, cand_pos, cand_e)
        ts = _rows_to_tiles(top_s, sub)
        te = _rows_to_tiles(top_e, sub)
        e0 = jnp.exp(ts[0] - top_s[0])
        e1 = jnp.exp(ts[1] - top_s[0])
        denom = _allreduce8(e0 + e1, jnp.add)
        row = pl.multiple_of(hd * PEER_TOPK, PEER_TOPK)
        idxt_ref[pl.ds(row, SUBLANES), lanes] = te[0] * ROW_WORDS
        idxt_ref[pl.ds(row + SUBLANES, SUBLANES), lanes] = te[1] * ROW_WORDS
        wts_ref[pl.ds(row, SUBLANES), lanes] = e0 / denom
        wts_ref[pl.ds(row + SUBLANES, SUBLANES), lanes] = e1 / denom
        return carry

    lax.fori_loop(0, PEER_HEADS * n_chunks, head_chunk, 0)
    idx_ref[...] = idxt_ref[...].T


def _const_spec(shape):
    return pl.BlockSpec(shape, lambda i: (0,) * len(shape))


def _peer_select(h2, w_q, sub_k1, sub_k2):
    n = h2.shape[0]
    bf16 = jnp.bfloat16
    pair_spec = pl.BlockSpec((PEER_PAIRS, SEL_TILE), lambda i: (0, i))
    return pl.pallas_call(
        _peer_select_kernel,
        grid=(n // SEL_TILE,),
        in_specs=[pl.BlockSpec((SEL_TILE, D_MODEL), lambda i: (i, 0)),
                  _const_spec((PEER_HEADS * PEER_QUERY_DIM, D_MODEL)),
                  _const_spec(sub_k1.shape), _const_spec(sub_k2.shape)],
        out_specs=[pl.BlockSpec((SEL_TILE, PEER_PAIRS), lambda i: (i, 0)), pair_spec],
        out_shape=[jax.ShapeDtypeStruct((n, PEER_PAIRS), jnp.int32), jax.ShapeDtypeStruct((PEER_PAIRS, n), jnp.float32)],
        scratch_shapes=[pltpu.VMEM((PEER_HEADS * PEER_QUERY_DIM, SEL_TILE), jnp.float32),
                        pltpu.VMEM((PEER_PAIRS, SEL_TILE), jnp.int32)],
        compiler_params=pltpu.CompilerParams(dimension_semantics=("arbitrary",), vmem_limit_bytes=VMEM_LIMIT_BYTES),
        name="peer_select",
    )(h2, w_q.T.astype(bf16), sub_k1.astype(bf16), sub_k2.astype(bf16))


ROW_TILE = 512
SMALL_WIDTH = LANES
PROJ_GROUPS = (("fox_qkv", 3 * FOX_WIDTH), ("pool_in", POOL_WIDTH), ("gdn_qkv", 3 * GDN_WIDTH),
               ("gdn_z", GDN_WIDTH), ("gates", N_BRANCHES * D_MODEL), ("small", SMALL_WIDTH))


PROJ_DTYPES = {"fox_qkv": jnp.bfloat16}


def _dense_params():
    return pltpu.CompilerParams(dimension_semantics=("arbitrary",), vmem_limit_bytes=VMEM_LIMIT_BYTES)


def _rearrange_w_in(w_in):
    offs = np.concatenate([[0], np.cumsum(IN_SPLITS)])
    fox_qkv, fox_f, pool_in, gdn_qkv, gdn_b, gdn_a, gdn_z, gates = [w_in[:, int(offs[i]):int(offs[i + 1])] for i in range(8)]
    small = jnp.concatenate([fox_f, gdn_b, gdn_a], axis=1)
    small = jnp.pad(small, ((0, 0), (0, SMALL_WIDTH - small.shape[1])))
    return jnp.concatenate([fox_qkv, pool_in, gdn_qkv, gdn_z, gates, small], axis=1).astype(jnp.bfloat16)


def _inproj_kernel(h_ref, w_ref, *out_refs):
    x = h_ref[...].astype(jnp.bfloat16)
    off = 0
    for (_, width), o_ref in zip(PROJ_GROUPS, out_refs):
        o_ref[...] = jnp.dot(x, w_ref[:, off:off + width], preferred_element_type=jnp.float32).astype(o_ref.dtype)
        off += width


def _inproj(h2, w_r):
    n = h2.shape[0]
    outs = pl.pallas_call(
        _inproj_kernel,
        grid=(n // ROW_TILE,),
        in_specs=[pl.BlockSpec((ROW_TILE, D_MODEL), lambda i: (i, 0)), _const_spec(w_r.shape)],
        out_specs=[pl.BlockSpec((ROW_TILE, w), lambda i: (i, 0)) for _, w in PROJ_GROUPS],
        out_shape=[jax.ShapeDtypeStruct((n, w), PROJ_DTYPES.get(name, jnp.float32)) for name, w in PROJ_GROUPS],
        compiler_params=_dense_params(),
        name="inproj",
    )(h2, w_r)
    return {name: o for (name, _), o in zip(PROJ_GROUPS, outs)}


def _layer_norm_rows(y, g, b):
    mu = jnp.mean(y, axis=-1, keepdims=True)
    d = y - mu
    var = jnp.mean(d * d, axis=-1, keepdims=True)
    return d * lax.rsqrt(var + LN_EPS) * g + b


def _merge_kernel(h_ref, a_ref, b_ref, c_ref, gates_ref, wb_ref, wo_ref, g_ref, beta_ref, o_ref):
    bf16 = jnp.bfloat16
    merged = jnp.zeros((ROW_TILE, D_MODEL), jnp.float32)
    for n, br_ref in enumerate((a_ref, b_ref, c_ref)):
        to_model = jnp.dot(br_ref[...].astype(bf16), wb_ref[n], preferred_element_type=jnp.float32)
        merged = merged + jax.nn.sigmoid(gates_ref[:, n * D_MODEL:(n + 1) * D_MODEL]) * to_model
    mix = jnp.dot(merged.astype(bf16), wo_ref[...], preferred_element_type=jnp.float32)
    o_ref[...] = _layer_norm_rows(DEEPNORM_ALPHA * h_ref[...] + mix, g_ref[...], beta_ref[...])


def _merge(h2, br_a, br_b, br_c, gates, w_branch, w_out, ln_g, ln_b):
    n = h2.shape[0]
    bf16 = jnp.bfloat16
    row = lambda w: pl.BlockSpec((ROW_TILE, w), lambda i: (i, 0))
    return pl.pallas_call(
        _merge_kernel,
        grid=(n // ROW_TILE,),
        in_specs=[row(D_MODEL), row(FOX_WIDTH), row(POOL_WIDTH), row(GDN_WIDTH), row(N_BRANCHES * D_MODEL),
                  _const_spec(w_branch.shape), _const_spec(w_out.shape), _const_spec((1, D_MODEL)), _const_spec((1, D_MODEL))],
        out_specs=row(D_MODEL),
        out_shape=jax.ShapeDtypeStruct((n, D_MODEL), jnp.float32),
        compiler_params=_dense_params(),
        name="merge_ln1",
    )(h2, br_a, br_b, br_c, gates, w_branch.astype(bf16), w_out.astype(bf16), ln_g.reshape(1, -1), ln_b.reshape(1, -1))


def _ple_kernel(h_ref, p_ref, ffn_ref, wg_ref, wp_ref, g_ref, beta_ref, o_ref):
    bf16 = jnp.bfloat16
    h = h_ref[...]
    gate = jax.nn.sigmoid(jnp.dot(h.astype(bf16), wg_ref[...], preferred_element_type=jnp.float32))
    emb = jnp.dot(p_ref[...].astype(bf16), wp_ref[...], preferred_element_type=jnp.float32)
    o_ref[...] = _layer_norm_rows(DEEPNORM_ALPHA * h + ffn_ref[...] + gate * emb, g_ref[...], beta_ref[...])


def _ple_ln(h2, p2, ffn, w_g, w_p, ln_g, ln_b):
    n = h2.shape[0]
    bf16 = jnp.bfloat16
    row = lambda w: pl.BlockSpec((ROW_TILE, w), lambda i: (i, 0))
    return pl.pallas_call(
        _ple_kernel,
        grid=(n // ROW_TILE,),
        in_specs=[row(D_MODEL), row(p2.shape[1]), row(D_MODEL), _const_spec(w_g.shape), _const_spec(w_p.shape),
                  _const_spec((1, D_MODEL)), _const_spec((1, D_MODEL))],
        out_specs=row(D_MODEL),
        out_shape=jax.ShapeDtypeStruct((n, D_MODEL), jnp.float32),
        compiler_params=_dense_params(),
        name="ple_ln2",
    )(h2, p2, ffn, w_g.astype(bf16), w_p.astype(bf16), ln_g.reshape(1, -1), ln_b.reshape(1, -1))


SEQ_TILE = 512
LANE_FOX = 0
LANE_BETA = FOX_HEADS
LANE_DECAY = FOX_HEADS + GDN_HEADS


def _split3_dot(tri, x):
    bf16 = jnp.bfloat16
    hi = x.astype(bf16)
    r1 = x - hi.astype(jnp.float32)
    mid = r1.astype(bf16)
    lo = (r1 - mid.astype(jnp.float32)).astype(bf16)
    dot = lambda t: jnp.dot(tri, t, preferred_element_type=jnp.float32)
    return dot(hi) + dot(mid) + dot(lo)


def _softplus(x):
    return jnp.maximum(x, 0.0) + jnp.log(1.0 + jnp.exp(-jnp.abs(x)))


def _gate_prep_kernel(small_ref, bias_ref, alog_ref, aux_ref, cumt_ref, carry_ref):
    @pl.when(pl.program_id(1) == 0)
    def _():
        carry_ref[...] = jnp.zeros_like(carry_ref)

    x = small_ref[0] + bias_ref[...]
    lane = lax.broadcasted_iota(jnp.int32, x.shape, 1)
    is_fox = lane < LANE_BETA
    is_beta = (lane >= LANE_BETA) & (lane < LANE_DECAY)
    log_f = -_softplus(-x)
    decay = -jnp.exp(alog_ref[...]) * _softplus(x)
    r = lax.broadcasted_iota(jnp.int32, (SEQ_TILE, SEQ_TILE), 0)
    c = lax.broadcasted_iota(jnp.int32, (SEQ_TILE, SEQ_TILE), 1)
    tri = (c <= r).astype(jnp.bfloat16)
    chunk_shift = GDN_CHUNK.bit_length() - 1
    tri_chunk = ((c <= r) & ((r >> chunk_shift) == (c >> chunk_shift))).astype(jnp.bfloat16)
    cum = _split3_dot(tri, jnp.where(is_fox, log_f, 0.0)) + carry_ref[...]
    carry_ref[...] = cum[SEQ_TILE - 1:SEQ_TILE, :]
    gcum = _split3_dot(tri_chunk, jnp.where(is_fox | is_beta, 0.0, decay))
    aux_ref[0] = jnp.where(is_fox, cum, jnp.where(is_beta, jax.nn.sigmoid(x), gcum))
    cumt_ref[0] = cum.T[:SUBLANES, :]


def _gate_prep(small, B, S, fox_fb, gdn_a_log, gdn_dt_bias):
    pad = lambda v, at: jnp.zeros((1, SMALL_WIDTH), jnp.float32).at[0, at:at + v.shape[0]].set(v)
    bias = pad(fox_fb, LANE_FOX) + pad(gdn_dt_bias, LANE_DECAY)
    alog = pad(gdn_a_log, LANE_DECAY)
    return pl.pallas_call(
        _gate_prep_kernel,
        grid=(B, S // SEQ_TILE),
        in_specs=[pl.BlockSpec((1, SEQ_TILE, SMALL_WIDTH), lambda b, s: (b, s, 0)),
                  pl.BlockSpec((1, SMALL_WIDTH), lambda b, s: (0, 0)), pl.BlockSpec((1, SMALL_WIDTH), lambda b, s: (0, 0))],
        out_specs=[pl.BlockSpec((1, SEQ_TILE, SMALL_WIDTH), lambda b, s: (b, s, 0)),
                   pl.BlockSpec((1, SUBLANES, SEQ_TILE), lambda b, s: (b, 0, s))],
        out_shape=[jax.ShapeDtypeStruct((B, S, SMALL_WIDTH), jnp.float32), jax.ShapeDtypeStruct((B, SUBLANES, S), jnp.float32)],
        scratch_shapes=[pltpu.VMEM((1, SMALL_WIDTH), jnp.float32)],
        compiler_params=pltpu.CompilerParams(dimension_semantics=("arbitrary", "arbitrary"), vmem_limit_bytes=VMEM_LIMIT_BYTES),
        name="gate_prep",
    )(small.reshape(B, S, SMALL_WIDTH), bias, alog)


ATT_Q = 256
ATT_K = 512
HEADS_PER_BLOCK = LANES // FOX_HEAD_DIM


def _fox_kernel(q_ref, k_ref, v_ref, cumt_ref, o_ref):
    bf16 = jnp.bfloat16
    hp = pl.program_id(1)
    qi = pl.program_id(2)
    q = q_ref[0] * jnp.asarray(FOX_HEAD_DIM ** -0.5, bf16)
    lane = lax.broadcasted_iota(jnp.int32, (ATT_Q, LANES), 1)
    sub_k = lax.broadcasted_iota(jnp.int32, (SUBLANES, ATT_K), 0)
    heads = [hp * HEADS_PER_BLOCK + a for a in range(HEADS_PER_BLOCK)]
    in_head = [(lane >= a * FOX_HEAD_DIM) & (lane < (a + 1) * FOX_HEAD_DIM) for a in range(HEADS_PER_BLOCK)]
    q_head = [jnp.where(in_head[a], q, jnp.zeros_like(q)) for a in range(HEADS_PER_BLOCK)]
    n_full = (qi * ATT_Q) // ATT_K
    q_off = qi * ATT_Q - n_full * ATT_K
    visible = (lax.broadcasted_iota(jnp.int32, (ATT_Q, ATT_K), 1)
               <= lax.broadcasted_iota(jnp.int32, (ATT_Q, ATT_K), 0) + q_off)

    def per_lane(vals):
        out = vals[-1]
        for a in range(HEADS_PER_BLOCK - 2, -1, -1):
            out = jnp.where(in_head[a], vals[a], out)
        return out

    def block(j, carry, masked):
        stats, acc = carry
        rows = pl.ds(pl.multiple_of(j * ATT_K, ATT_K), ATT_K)
        k = k_ref[0, rows, :]
        v = v_ref[0, rows, :]
        cum_k = cumt_ref[0, :, rows]
        new_stats, alphas, pvs = [], [], []
        for a in range(HEADS_PER_BLOCK):
            m, l = stats[a]
            s = lax.dot_general(q_head[a], k, (((1,), (1,)), ((), ())), preferred_element_type=jnp.float32)
            s = s - jnp.sum(jnp.where(sub_k == heads[a], cum_k, 0.0), axis=0, keepdims=True)
            if masked:
                s = jnp.where(visible, s, -jnp.inf)
            m_new = jnp.maximum(m, jnp.max(s, axis=1, keepdims=True))
            alpha = jnp.exp(m - m_new)
            p = jnp.exp(s - m_new)
            new_stats.append((m_new, alpha * l + jnp.sum(p, axis=1, keepdims=True)))
            alphas.append(alpha)
            pvs.append(jnp.dot(p.astype(bf16), v, preferred_element_type=jnp.float32))
        return tuple(new_stats), per_lane(alphas) * acc + per_lane(pvs)

    stat0 = (jnp.full((ATT_Q, 1), -jnp.inf, jnp.float32), jnp.zeros((ATT_Q, 1), jnp.float32))
    init = ((stat0,) * HEADS_PER_BLOCK, jnp.zeros((ATT_Q, LANES), jnp.float32))
    carry = lax.fori_loop(0, n_full, functools.partial(block, masked=False), init)
    stats, acc = block(n_full, carry, masked=True)
    o_ref[0] = acc / per_lane([l for _, l in stats])


def _fox_attention(fox_qkv, cumt, B, S):
    qkv = fox_qkv.reshape(B, S, 3 * FOX_WIDTH)
    blocks = FOX_WIDTH // LANES
    out = pl.pallas_call(
        _fox_kernel,
        grid=(B, blocks, S // ATT_Q),
        in_specs=[pl.BlockSpec((1, ATT_Q, LANES), lambda b, hp, qi: (b, qi, hp)),
                  pl.BlockSpec((1, S, LANES), lambda b, hp, qi: (b, 0, blocks + hp)),
                  pl.BlockSpec((1, S, LANES), lambda b, hp, qi: (b, 0, 2 * blocks + hp)),
                  pl.BlockSpec((1, SUBLANES, S), lambda b, hp, qi: (b, 0, 0))],
        out_specs=pl.BlockSpec((1, ATT_Q, LANES), lambda b, hp, qi: (b, qi, hp)),
        out_shape=jax.ShapeDtypeStruct((B, S, FOX_WIDTH), jnp.float32),
        compiler_params=pltpu.CompilerParams(dimension_semantics=("arbitrary",) * 3, vmem_limit_bytes=VMEM_LIMIT_BYTES),
        name="fox_attention",
    )(qkv, qkv, qkv, cumt)
    return out.reshape(B * S, FOX_WIDTH)


POOL_HALO = SUBLANES * 2


def _shift_rows(x, k):
    return pltpu.roll(x, k, 0)


def _pool_kernel(x_ref, w_ref, scale_ref, o_ref, halo_ref):
    @pl.when(pl.program_id(1) == 0)
    def _():
        halo_ref[...] = jnp.zeros_like(halo_ref)

    s_blk = pl.program_id(1)
    rows = POOL_HALO + SEQ_TILE
    row = lax.broadcasted_iota(jnp.int32, (rows, POOL_GROUP_DIM), 0)
    t_abs = s_blk * SEQ_TILE + row - POOL_HALO
    for g, window in enumerate(POOL_WINDOWS):
        cols = slice(g * POOL_GROUP_DIM, (g + 1) * POOL_GROUP_DIM)
        x = x_ref[0, :, cols]
        ext = jnp.concatenate([halo_ref[:, cols], x], axis=0)
        acc = ext
        span = 1
        while span < window:
            acc = acc + jnp.where(row >= span, _shift_rows(acc, span), 0.0)
            span *= 2
        win_sum = acc[POOL_HALO:, :]
        count = jnp.minimum(t_abs[POOL_HALO:, :] + 1, window).astype(jnp.float32)
        pooled = win_sum / count - x
        mixed = jnp.dot(pooled.astype(jnp.bfloat16), w_ref[g], preferred_element_type=jnp.float32)
        o_ref[0, :, cols] = mixed * scale_ref[:, cols]
    halo_ref[...] = x_ref[0, SEQ_TILE - POOL_HALO:, :]


def _multiscale_pool(pool_in, w_pool, pool_scale, B, S):
    out = pl.pallas_call(
        _pool_kernel,
        grid=(B, S // SEQ_TILE),
        in_specs=[pl.BlockSpec((1, SEQ_TILE, POOL_WIDTH), lambda b, s: (b, s, 0)),
                  pl.BlockSpec(w_pool.shape, lambda b, s: (0, 0, 0)), pl.BlockSpec((1, POOL_WIDTH), lambda b, s: (0, 0))],
        out_specs=pl.BlockSpec((1, SEQ_TILE, POOL_WIDTH), lambda b, s: (b, s, 0)),
        out_shape=jax.ShapeDtypeStruct((B, S, POOL_WIDTH), jnp.float32),
        scratch_shapes=[pltpu.VMEM((POOL_HALO, POOL_WIDTH), jnp.float32)],
        compiler_params=pltpu.CompilerParams(dimension_semantics=("arbitrary", "arbitrary"), vmem_limit_bytes=VMEM_LIMIT_BYTES),
        name="pool",
    )(pool_in.reshape(B, S, POOL_WIDTH), w_pool.astype(jnp.bfloat16), pool_scale.reshape(1, POOL_WIDTH))
    return out.reshape(B * S, POOL_WIDTH)


GDN_TILE_CHUNKS = SEQ_TILE // GDN_CHUNK
GDN_HALO = SUBLANES


def _l2norm(x):
    return x * lax.rsqrt(jnp.sum(x * x, axis=-1, keepdims=True) + L2_EPS)


def _silu(x):
    return x * jax.nn.sigmoid(x)


def _bdot(a, b, dims):
    return lax.dot_general(a.astype(jnp.bfloat16), b.astype(jnp.bfloat16), ((dims[0], dims[1]), ((0,), (0,))),
                           preferred_element_type=jnp.float32)


def _bdot3(a, b):
    bf16 = jnp.bfloat16
    a_hi, b_hi = a.astype(bf16), b.astype(bf16)
    a_lo = (a - a_hi.astype(jnp.float32)).astype(bf16)
    b_lo = (b - b_hi.astype(jnp.float32)).astype(bf16)
    dot = lambda x, y: lax.dot_general(x, y, (((2,), (1,)), ((0,), (0,))), preferred_element_type=jnp.float32)
    return dot(a_hi, b_hi) + dot(a_hi, b_lo) + dot(a_lo, b_hi)


def _unit_lower_inverse(low):
    c = low.shape[-1]
    eye = (lax.broadcasted_iota(jnp.int32, (c, c), 0) == lax.broadcasted_iota(jnp.int32, (c, c), 1)).astype(jnp.float32)
    inv = eye[None] - low
    power = low
    span = 2
    while span < c:
        power = _bdot3(power, power)
        inv = inv + _bdot3(inv, power)
        span *= 2
    return inv


def _gdn_kernel(qkv_ref, z_ref, aux_ref, convw_ref, norm_ref, o_ref, halo_ref, state_ref):
    @pl.when(pl.program_id(1) == 0)
    def _():
        halo_ref[...] = jnp.zeros_like(halo_ref)
        state_ref[...] = jnp.zeros_like(state_ref)

    nc, c, d = GDN_TILE_CHUNKS, GDN_CHUNK, GDN_HEAD_DIM
    x = qkv_ref[0]
    ext = jnp.concatenate([halo_ref[...], x], axis=0)
    conv = ext * convw_ref[GDN_CONV - 1:GDN_CONV, :]
    for k in range(1, GDN_CONV):
        conv = conv + pltpu.roll(ext, k, 0) * convw_ref[GDN_CONV - 1 - k:GDN_CONV - k, :]
    act = _silu(conv[GDN_HALO:, :])
    halo_ref[...] = x[SEQ_TILE - GDN_HALO:, :]

    aux = aux_ref[0]
    aux_t = aux.T
    lane = lax.broadcasted_iota(jnp.int32, aux.shape, 1)
    ri = lax.broadcasted_iota(jnp.int32, (c, c), 0)
    ci = lax.broadcasted_iota(jnp.int32, (c, c), 1)
    for hd in range(GDN_HEADS):
        cols = slice(hd * d, (hd + 1) * d)
        q = _l2norm(act[:, cols]) * (d ** -0.5)
        k = _l2norm(act[:, GDN_WIDTH + hd * d:GDN_WIDTH + (hd + 1) * d])
        v = act[:, 2 * GDN_WIDTH + hd * d:2 * GDN_WIDTH + (hd + 1) * d]
        beta = jnp.sum(jnp.where(lane == LANE_BETA + hd, aux, 0.0), axis=1, keepdims=True)
        g_col = jnp.sum(jnp.where(lane == LANE_DECAY + hd, aux, 0.0), axis=1, keepdims=True)
        g_row = aux_t[LANE_DECAY + hd:LANE_DECAY + hd + 1, :]
        diff = jnp.stack([g_col[i * c:(i + 1) * c, :] - g_row[:, i * c:(i + 1) * c] for i in range(nc)])
        decay_strict = jnp.exp(jnp.where((ri > ci)[None], diff, -jnp.inf))
        decay_incl = jnp.exp(jnp.where((ri >= ci)[None], diff, -jnp.inf))
        k_beta = k * beta
        q3, k3 = q.reshape(nc, c, d), k.reshape(nc, c, d)
        low = _bdot(k_beta.reshape(nc, c, d), k3, ((2,), (2,))) * decay_strict
        t_inv = _unit_lower_inverse(low)
        u = _bdot(t_inv, (v * beta).reshape(nc, c, d), ((2,), (1,)))
        w = _bdot(t_inv, (k_beta * jnp.exp(g_col)).reshape(nc, c, d), ((2,), (1,)))
        a_intra = _bdot(q3, k3, ((2,), (2,))) * decay_incl
        q_dec = (q * jnp.exp(g_col)).reshape(nc, c, d)
        g3 = g_col.reshape(nc, c, 1)
        g_last = g3[:, c - 1:c, :]
        k_dec = k3 * jnp.exp(g_last - g3)
        state = state_ref[hd]
        outs = []
        for i in range(nc):
            bf16 = jnp.bfloat16
            state_b = state.astype(bf16)
            v_new = u[i] - jnp.dot(w[i].astype(bf16), state_b, preferred_element_type=jnp.float32)
            v_new_b = v_new.astype(bf16)
            outs.append(jnp.dot(q_dec[i].astype(bf16), state_b, preferred_element_type=jnp.float32)
                        + jnp.dot(a_intra[i].astype(bf16), v_new_b, preferred_element_type=jnp.float32))
            state = state * jnp.exp(g_last[i]) + lax.dot_general(
                k_dec[i].astype(bf16), v_new_b, (((0,), (0,)), ((), ())), preferred_element_type=jnp.float32)
        state_ref[hd] = state
        o = jnp.concatenate(outs, axis=0)
        o = o * lax.rsqrt(jnp.mean(o * o, axis=-1, keepdims=True) + RMS_EPS) * norm_ref[...]
        o_ref[0, :, cols] = o * _silu(z_ref[0, :, cols])


def _gated_deltanet(gdn_qkv, gdn_z, aux, conv_w, norm_g, B, S):
    width = 3 * GDN_WIDTH
    out = pl.pallas_call(
        _gdn_kernel,
        grid=(B, S // SEQ_TILE),
        in_specs=[pl.BlockSpec((1, SEQ_TILE, width), lambda b, s: (b, s, 0)),
                  pl.BlockSpec((1, SEQ_TILE, GDN_WIDTH), lambda b, s: (b, s, 0)),
                  pl.BlockSpec((1, SEQ_TILE, SMALL_WIDTH), lambda b, s: (b, s, 0)),
                  pl.BlockSpec((GDN_CONV, width), lambda b, s: (0, 0)),
                  pl.BlockSpec((1, GDN_HEAD_DIM), lambda b, s: (0, 0))],
        out_specs=pl.BlockSpec((1, SEQ_TILE, GDN_WIDTH), lambda b, s: (b, s, 0)),
        out_shape=jax.ShapeDtypeStruct((B, S, GDN_WIDTH), jnp.float32),
        scratch_shapes=[pltpu.VMEM((GDN_HALO, width), jnp.float32),
                        pltpu.VMEM((GDN_HEADS, GDN_HEAD_DIM, GDN_HEAD_DIM), jnp.float32)],
        compiler_params=pltpu.CompilerParams(dimension_semantics=("arbitrary", "arbitrary"), vmem_limit_bytes=VMEM_LIMIT_BYTES),
        name="gated_deltanet",
    )(gdn_qkv.reshape(B, S, width), gdn_z.reshape(B, S, GDN_WIDTH), aux, conv_w.reshape(GDN_CONV, width),
      norm_g.reshape(1, GDN_HEAD_DIM))
    return out.reshape(B * S, GDN_WIDTH)


def _mixer_branches(proj, B, S, fox_fb, pool_w, pool_scale, gdn_conv, gdn_a_log, gdn_dt_bias, gdn_norm):
    aux, cumt = _gate_prep(proj["small"], B, S, fox_fb, gdn_a_log, gdn_dt_bias)
    branch_a = _fox_attention(proj["fox_qkv"], cumt, B, S)
    branch_b = _multiscale_pool(proj["pool_in"], pool_w, pool_scale, B, S)
    branch_c = _gated_deltanet(proj["gdn_qkv"], proj["gdn_z"], aux, gdn_conv, gdn_norm, B, S)
    return branch_a, branch_b, branch_c


def kernel(x, p, w_in, fox_fb, pool_w, pool_scale, gdn_conv, gdn_a_log, gdn_dt_bias, gdn_norm, w_branch, w_out, ln1_g, ln1_b, peer_wq, peer_k1, peer_k2, peer_u, peer_v, ple_gate, ple_proj, ln2_g, ln2_b):
    B, S, D = x.shape
    h2 = x.reshape(B * S, D)
    for i in range(DEPTH):
        proj = _inproj(h2, _rearrange_w_in(w_in[i]))
        br_a, br_b, br_c = _mixer_branches(proj, B, S, fox_fb[i], pool_w[i], pool_scale[i], gdn_conv[i], gdn_a_log[i],
                                           gdn_dt_bias[i], gdn_norm[i])
        h2 = _merge(h2, br_a, br_b, br_c, proj["gates"], w_branch[i], w_out[i], ln1_g[i], ln1_b[i])
        idx_t, wts_t = _peer_select(h2, peer_wq[i], peer_k1[i], peer_k2[i])
        ffn = _peer_retrieve(h2, idx_t, wts_t, _pack_table(peer_u[i]), _pack_table(peer_v[i]))
        h2 = _ple_ln(h2, p[i].reshape(B * S, -1), ffn, ple_gate[i], ple_proj[i], ln2_g[i], ln2_b[i])
    return h2.reshape(B, S, D)
```

```python
import functools
import math

import jax
import jax.numpy as jnp
import numpy as np
from jax import lax
from jax.experimental import pallas as pl
from jax.experimental.pallas import tpu as pltpu

D_MODEL = 1024
DEPTH = 2
FOX_HEADS = 8
FOX_HEAD_DIM = 64
FOX_WIDTH = FOX_HEADS * FOX_HEAD_DIM
Q_BLOCK = 128
POOL_WINDOWS = (2, 4, 8, 16)
POOL_GROUPS = 4
POOL_GROUP_DIM = 128
POOL_WIDTH = POOL_GROUPS * POOL_GROUP_DIM
GDN_HEADS = 4
GDN_HEAD_DIM = 128
GDN_WIDTH = GDN_HEADS * GDN_HEAD_DIM
GDN_CONV = 4
GDN_CHUNK = 64
N_BRANCHES = 3
IN_SPLITS = (3 * FOX_WIDTH, FOX_HEADS, POOL_WIDTH, 3 * GDN_WIDTH, GDN_HEADS, GDN_HEADS, GDN_WIDTH, N_BRANCHES * D_MODEL)
PEER_HEADS = 8
PEER_KEYS = 128
PEER_N_EXPERTS = PEER_KEYS * PEER_KEYS
PEER_QUERY_DIM = 256
PEER_HALF = PEER_QUERY_DIM // 2
PEER_TOPK = 16
PEER_PAIRS = PEER_HEADS * PEER_TOPK
DEEPNORM_ALPHA = (2 * DEPTH) ** 0.25
LN_EPS = 1e-5
RMS_EPS = 1e-6
L2_EPS = 1e-6

LANES = 128
SUBLANES = 8
VMEM_LIMIT_BYTES = 52 * 1024 * 1024

PEER_TOKEN_TILE = 128
PEER_UNROLL = 2
ROW_WORDS = D_MODEL // 2 // LANES
PLANE_STRIDE = PEER_PAIRS + 1
PLANE_ROWS = (ROW_WORDS - 1) * PLANE_STRIDE + PEER_PAIRS


PACK_ROWS = 512


def _pack_kernel(tab_ref, out_ref):
    def bf16_bits(x):
        return lax.bitcast_convert_type(x.astype(jnp.bfloat16).astype(jnp.float32), jnp.int32)

    for s in range(ROW_WORDS):
        lo = bf16_bits(tab_ref[:, s * LANES:(s + 1) * LANES])
        hi = bf16_bits(tab_ref[:, (s + ROW_WORDS) * LANES:(s + ROW_WORDS + 1) * LANES])
        out_ref[pl.ds(s, PACK_ROWS, stride=ROW_WORDS), :] = lax.shift_right_logical(lo, 16) | hi


def _pack_table(tab):
    e = tab.shape[0]
    return pl.pallas_call(
        _pack_kernel,
        grid=(e // PACK_ROWS,),
        in_specs=[pl.BlockSpec((PACK_ROWS, D_MODEL), lambda i: (i, 0))],
        out_specs=pl.BlockSpec((PACK_ROWS * ROW_WORDS, LANES), lambda i: (i, 0)),
        out_shape=jax.ShapeDtypeStruct((e * ROW_WORDS, LANES), jnp.int32),
        compiler_params=pltpu.CompilerParams(dimension_semantics=("arbitrary",), vmem_limit_bytes=VMEM_LIMIT_BYTES),
        name="pack_table",
    )(tab)


def _unpack_lo(w):
    return lax.bitcast_convert_type(w << 16, jnp.float32)


def _unpack_hi(w):
    return lax.bitcast_convert_type(w & jnp.int32(-65536), jnp.float32)


def _gather_rows(idx_ref, tab_ref, tile_ref, t):
    token_idx = idx_ref.at[t]
    for p in range(PEER_PAIRS):
        row = pl.multiple_of(token_idx[p], ROW_WORDS)
        tile_ref[pl.ds(p, ROW_WORDS, stride=PLANE_STRIDE), :] = tab_ref[pl.ds(row, ROW_WORDS), :]


def _token_pairs(idx_ref, tab_ref, tiles, compute, init):
    _gather_rows(idx_ref, tab_ref, tiles[0], 0)

    def group(i, carry):
        for k in range(PEER_UNROLL):
            t = PEER_UNROLL * i + k
            _gather_rows(idx_ref, tab_ref, tiles[(k + 1) % 2], jnp.minimum(t + 1, PEER_TOKEN_TILE - 1))
            carry = compute(t, tiles[k % 2], carry)
        return carry

    return lax.fori_loop(0, PEER_TOKEN_TILE // PEER_UNROLL, group, init)


def _peer_u_kernel(idx_ref, x_ref, wt_ref, tab_ref, ct_ref, tile0_ref, tile1_ref):
    lane = lax.broadcasted_iota(jnp.int32, (PEER_PAIRS, PEER_TOKEN_TILE), 1)

    def lane_sum_into(act_t, partial, t):
        return jnp.where(lane == t, jnp.sum(partial, axis=1, keepdims=True), act_t)

    def token(t, tile_ref, carry):
        act_t, prev = carry
        act_t = lane_sum_into(act_t, prev, t - 1)
        x = x_ref[t]
        acc = jnp.zeros((PEER_PAIRS, LANES), jnp.float32)
        for s in range(ROW_WORDS):
            w = tile_ref[pl.ds(s * PLANE_STRIDE, PEER_PAIRS), :]
            acc = acc + _unpack_lo(w) * x[s:s + 1, :] + _unpack_hi(w) * x[s + ROW_WORDS:s + ROW_WORDS + 1, :]
        return act_t, acc

    act_t, last = _token_pairs(idx_ref, tab_ref, (tile0_ref, tile1_ref), token,
                               (jnp.zeros((PEER_PAIRS, PEER_TOKEN_TILE), jnp.float32),
                                jnp.zeros((PEER_PAIRS, LANES), jnp.float32)))
    act_t = lane_sum_into(act_t, last, PEER_TOKEN_TILE - 1)
    gelu = 0.5 * act_t * (1.0 + lax.erf(act_t * (2.0 ** -0.5)))
    ct_ref[...] = gelu * wt_ref[...]


def _peer_v_kernel(idx_ref, ct_ref, tab_ref, out_ref, tile0_ref, tile1_ref):
    lane = lax.broadcasted_iota(jnp.int32, (PEER_PAIRS, PEER_TOKEN_TILE), 1)

    def token(t, tile_ref, carry):
        c = jnp.sum(jnp.where(lane == t, ct_ref[...], 0.0), axis=1, keepdims=True)
        lo, hi = [], []
        for s in range(ROW_WORDS):
            w = tile_ref[pl.ds(s * PLANE_STRIDE, PEER_PAIRS), :]
            lo.append(jnp.sum(_unpack_lo(w) * c, axis=0, keepdims=True))
            hi.append(jnp.sum(_unpack_hi(w) * c, axis=0, keepdims=True))
        out_ref[t] = jnp.concatenate(lo + hi, axis=0)
        return carry

    _token_pairs(idx_ref, tab_ref, (tile0_ref, tile1_ref), token, 0)


def _resident_spec(shape):
    return pl.BlockSpec(shape, lambda i: (0,) * len(shape), pipeline_mode=pl.Buffered(1))


def _peer_retrieve(x, idx, wts_t, u_packed, v_packed):
    n = x.shape[0]
    grid = (n // PEER_TOKEN_TILE,)
    params = pltpu.CompilerParams(dimension_semantics=("arbitrary",), vmem_limit_bytes=VMEM_LIMIT_BYTES)
    idx_spec = pl.BlockSpec((PEER_TOKEN_TILE, PEER_PAIRS), lambda i: (i, 0), memory_space=pltpu.SMEM)
    pair_spec = pl.BlockSpec((PEER_PAIRS, PEER_TOKEN_TILE), lambda i: (0, i))
    row_spec = pl.BlockSpec((PEER_TOKEN_TILE, SUBLANES, LANES), lambda i: (i, 0, 0))
    tiles = [pltpu.VMEM((PLANE_ROWS, LANES), jnp.int32)] * 2
    ct = pl.pallas_call(
        _peer_u_kernel,
        grid=grid,
        in_specs=[idx_spec, row_spec, pair_spec, _resident_spec(u_packed.shape)],
        out_specs=pair_spec,
        out_shape=jax.ShapeDtypeStruct((PEER_PAIRS, n), jnp.float32),
        scratch_shapes=tiles,
        compiler_params=params,
        name="peer_u",
    )(idx, x.reshape(n, SUBLANES, LANES), wts_t, u_packed)
    out = pl.pallas_call(
        _peer_v_kernel,
        grid=grid,
        in_specs=[idx_spec, pair_spec, _resident_spec(v_packed.shape)],
        out_specs=row_spec,
        out_shape=jax.ShapeDtypeStruct((n, SUBLANES, LANES), jnp.float32),
        scratch_shapes=tiles,
        compiler_params=params,
        name="peer_v",
    )(idx, ct, v_packed)
    return out.reshape(n, D_MODEL)


SEL_TILE = 256
SEL_CHUNK = 2 * LANES
_BIG_POS = 1 << 30
CAND_TILES = tuple((a, bh) for a in range(PEER_TOPK) for bh in range(2) if (a + 1) * (bh * SUBLANES + 1) <= PEER_TOPK)


def _allreduce8(x, op):
    for shift in (4, 2, 1):
        x = op(x, pltpu.roll(x, shift, 0))
    return x


def _top16(s3, pos3):
    vals, sels = [], []
    for _ in range(PEER_TOPK):
        m = _allreduce8(jnp.max(s3, axis=0), jnp.maximum)
        sel = _allreduce8(jnp.min(jnp.where(s3 == m[None], pos3, _BIG_POS), axis=0), jnp.minimum)
        s3 = jnp.where(pos3 == sel[None], -jnp.inf, s3)
        vals.append(m)
        sels.append(sel)
    return vals, sels


def _rows_to_tiles(rows, sub):
    tiles = []
    for half in range(2):
        t = rows[half * SUBLANES]
        for k in range(1, SUBLANES):
            t = jnp.where(sub == k, rows[half * SUBLANES + k], t)
        tiles.append(t)
    return tiles


def _peer_select_kernel(h_ref, wqt_ref, k1_ref, k2_ref, idx_ref, wts_ref, qt_ref, idxt_ref):
    bf16 = jnp.bfloat16
    qt_ref[...] = lax.dot_general(wqt_ref[...], h_ref[...].astype(bf16), (((1,), (1,)), ((), ())),
                                  preferred_element_type=jnp.float32)
    n_chunks = SEL_TILE // SEL_CHUNK
    g1 = PEER_KEYS // SUBLANES
    key_pos = (lax.broadcasted_iota(jnp.int32, (g1, SUBLANES, SEL_CHUNK), 0) * SUBLANES
               + lax.broadcasted_iota(jnp.int32, (g1, SUBLANES, SEL_CHUNK), 1))
    sub = lax.broadcasted_iota(jnp.int32, (SUBLANES, SEL_CHUNK), 0)

    def head_chunk(j, carry):
        hd = j // n_chunks
        lanes = pl.ds(pl.multiple_of((j % n_chunks) * SEL_CHUNK, SEL_CHUNK), SEL_CHUNK)
        q_row = pl.multiple_of(hd * PEER_QUERY_DIM, PEER_QUERY_DIM)
        q1 = qt_ref[pl.ds(q_row, PEER_HALF), lanes]
        q2 = qt_ref[pl.ds(q_row + PEER_HALF, PEER_HALF), lanes]
        s1 = jnp.dot(k1_ref[hd], q1.astype(bf16), preferred_element_type=jnp.float32)
        s2 = jnp.dot(k2_ref[hd], q2.astype(bf16), preferred_element_type=jnp.float32)
        v1, i1 = _top16(s1.reshape(g1, SUBLANES, SEL_CHUNK), key_pos)
        v2, i2 = _top16(s2.reshape(g1, SUBLANES, SEL_CHUNK), key_pos)
        v2t = _rows_to_tiles(v2, sub)
        i2t = _rows_to_tiles(i2, sub)
        cand = jnp.stack([v1[a] + v2t[bh] for a, bh in CAND_TILES])
        cand_key = jnp.stack([(a * PEER_TOPK + bh * SUBLANES) * PEER_N_EXPERTS + sub * PEER_N_EXPERTS
                              + i1[a] * PEER_KEYS + i2t[bh] for a, bh in CAND_TILES])
        top_s, top_key = _top16(cand, cand_key)
        ts = _rows_to_tiles(top_s, sub)
        te = [key & (PEER_N_EXPERTS - 1) for key in _rows_to_tiles(top_key, sub)]
        e0 = jnp.exp(ts[0] - top_s[0])
        e1 = jnp.exp(ts[1] - top_s[0])
        denom = _allreduce8(e0 + e1, jnp.add)
        row = pl.multiple_of(hd * PEER_TOPK, PEER_TOPK)
        idxt_ref[pl.ds(row, SUBLANES), lanes] = te[0] * ROW_WORDS
        idxt_ref[pl.ds(row + SUBLANES, SUBLANES), lanes] = te[1] * ROW_WORDS
        wts_ref[pl.ds(row, SUBLANES), lanes] = e0 / denom
        wts_ref[pl.ds(row + SUBLANES, SUBLANES), lanes] = e1 / denom
        return carry

    lax.fori_loop(0, PEER_HEADS * n_chunks, head_chunk, 0)
    idx_ref[...] = idxt_ref[...].T


def _const_spec(shape):
    return pl.BlockSpec(shape, lambda i: (0,) * len(shape))


def _peer_select(h2, w_q, sub_k1, sub_k2):
    n = h2.shape[0]
    bf16 = jnp.bfloat16
    pair_spec = pl.BlockSpec((PEER_PAIRS, SEL_TILE), lambda i: (0, i))
    return pl.pallas_call(
        _peer_select_kernel,
        grid=(n // SEL_TILE,),
        in_specs=[pl.BlockSpec((SEL_TILE, D_MODEL), lambda i: (i, 0)),
                  _const_spec((PEER_HEADS * PEER_QUERY_DIM, D_MODEL)),
                  _const_spec(sub_k1.shape), _const_spec(sub_k2.shape)],
        out_specs=[pl.BlockSpec((SEL_TILE, PEER_PAIRS), lambda i: (i, 0)), pair_spec],
        out_shape=[jax.ShapeDtypeStruct((n, PEER_PAIRS), jnp.int32), jax.ShapeDtypeStruct((PEER_PAIRS, n), jnp.float32)],
        scratch_shapes=[pltpu.VMEM((PEER_HEADS * PEER_QUERY_DIM, SEL_TILE), jnp.float32),
                        pltpu.VMEM((PEER_PAIRS, SEL_TILE), jnp.int32)],
        compiler_params=pltpu.CompilerParams(dimension_semantics=("arbitrary",), vmem_limit_bytes=VMEM_LIMIT_BYTES),
        name="peer_select",
    )(h2, w_q.T.astype(bf16), sub_k1.astype(bf16), sub_k2.astype(bf16))


ROW_TILE = 512
SMALL_WIDTH = LANES
PROJ_GROUPS = (("fox_qkv", 3 * FOX_WIDTH), ("pool_in", POOL_WIDTH), ("gdn_qkv", 3 * GDN_WIDTH),
               ("gdn_z", GDN_WIDTH), ("gates", N_BRANCHES * D_MODEL), ("small", SMALL_WIDTH))


PROJ_DTYPES = {"fox_qkv": jnp.bfloat16}


def _dense_params():
    return pltpu.CompilerParams(dimension_semantics=("arbitrary",), vmem_limit_bytes=VMEM_LIMIT_BYTES)


def _rearrange_w_in(w_in):
    offs = np.concatenate([[0], np.cumsum(IN_SPLITS)])
    fox_qkv, fox_f, pool_in, gdn_qkv, gdn_b, gdn_a, gdn_z, gates = [w_in[:, int(offs[i]):int(offs[i + 1])] for i in range(8)]
    small = jnp.concatenate([fox_f, gdn_b, gdn_a], axis=1)
    small = jnp.pad(small, ((0, 0), (0, SMALL_WIDTH - small.shape[1])))
    return jnp.concatenate([fox_qkv, pool_in, gdn_qkv, gdn_z, gates, small], axis=1).astype(jnp.bfloat16)


def _inproj_kernel(h_ref, w_ref, *out_refs):
    x = h_ref[...].astype(jnp.bfloat16)
    off = 0
    for (_, width), o_ref in zip(PROJ_GROUPS, out_refs):
        o_ref[...] = jnp.dot(x, w_ref[:, off:off + width], preferred_element_type=jnp.float32).astype(o_ref.dtype)
        off += width


def _inproj(h2, w_r):
    n = h2.shape[0]
    outs = pl.pallas_call(
        _inproj_kernel,
        grid=(n // ROW_TILE,),
        in_specs=[pl.BlockSpec((ROW_TILE, D_MODEL), lambda i: (i, 0)), _const_spec(w_r.shape)],
        out_specs=[pl.BlockSpec((ROW_TILE, w), lambda i: (i, 0)) for _, w in PROJ_GROUPS],
        out_shape=[jax.ShapeDtypeStruct((n, w), PROJ_DTYPES.get(name, jnp.float32)) for name, w in PROJ_GROUPS],
        compiler_params=_dense_params(),
        name="inproj",
    )(h2, w_r)
    return {name: o for (name, _), o in zip(PROJ_GROUPS, outs)}


def _layer_norm_rows(y, g, b):
    mu = jnp.mean(y, axis=-1, keepdims=True)
    d = y - mu
    var = jnp.mean(d * d, axis=-1, keepdims=True)
    return d * lax.rsqrt(var + LN_EPS) * g + b


def _merge_kernel(h_ref, a_ref, b_ref, c_ref, gates_ref, wb_ref, wo_ref, g_ref, beta_ref, o_ref):
    bf16 = jnp.bfloat16
    merged = jnp.zeros((ROW_TILE, D_MODEL), jnp.float32)
    for n, br_ref in enumerate((a_ref, b_ref, c_ref)):
        to_model = jnp.dot(br_ref[...].astype(bf16), wb_ref[n], preferred_element_type=jnp.float32)
        merged = merged + jax.nn.sigmoid(gates_ref[:, n * D_MODEL:(n + 1) * D_MODEL]) * to_model
    mix = jnp.dot(merged.astype(bf16), wo_ref[...], preferred_element_type=jnp.float32)
    o_ref[...] = _layer_norm_rows(DEEPNORM_ALPHA * h_ref[...] + mix, g_ref[...], beta_ref[...])


def _merge(h2, br_a, br_b, br_c, gates, w_branch, w_out, ln_g, ln_b):
    n = h2.shape[0]
    bf16 = jnp.bfloat16
    row = lambda w: pl.BlockSpec((ROW_TILE, w), lambda i: (i, 0))
    return pl.pallas_call(
        _merge_kernel,
        grid=(n // ROW_TILE,),
        in_specs=[row(D_MODEL), row(FOX_WIDTH), row(POOL_WIDTH), row(GDN_WIDTH), row(N_BRANCHES * D_MODEL),
                  _const_spec(w_branch.shape), _const_spec(w_out.shape), _const_spec((1, D_MODEL)), _const_spec((1, D_MODEL))],
        out_specs=row(D_MODEL),
        out_shape=jax.ShapeDtypeStruct((n, D_MODEL), jnp.float32),
        compiler_params=_dense_params(),
        name="merge_ln1",
    )(h2, br_a, br_b, br_c, gates, w_branch.astype(bf16), w_out.astype(bf16), ln_g.reshape(1, -1), ln_b.reshape(1, -1))


def _ple_kernel(h_ref, p_ref, ffn_ref, wg_ref, wp_ref, g_ref, beta_ref, o_ref):
    bf16 = jnp.bfloat16
    h = h_ref[...]
    gate = jax.nn.sigmoid(jnp.dot(h.astype(bf16), wg_ref[...], preferred_element_type=jnp.float32))
    emb = jnp.dot(p_ref[...].astype(bf16), wp_ref[...], preferred_element_type=jnp.float32)
    o_ref[...] = _layer_norm_rows(DEEPNORM_ALPHA * h + ffn_ref[...] + gate * emb, g_ref[...], beta_ref[...])


def _ple_ln(h2, p2, ffn, w_g, w_p, ln_g, ln_b):
    n = h2.shape[0]
    bf16 = jnp.bfloat16
    row = lambda w: pl.BlockSpec((ROW_TILE, w), lambda i: (i, 0))
    return pl.pallas_call(
        _ple_kernel,
        grid=(n // ROW_TILE,),
        in_specs=[row(D_MODEL), row(p2.shape[1]), row(D_MODEL), _const_spec(w_g.shape), _const_spec(w_p.shape),
                  _const_spec((1, D_MODEL)), _const_spec((1, D_MODEL))],
        out_specs=row(D_MODEL),
        out_shape=jax.ShapeDtypeStruct((n, D_MODEL), jnp.float32),
        compiler_params=_dense_params(),
        name="ple_ln2",
    )(h2, p2, ffn, w_g.astype(bf16), w_p.astype(bf16), ln_g.reshape(1, -1), ln_b.reshape(1, -1))


SEQ_TILE = 512
LANE_FOX = 0
LANE_BETA = FOX_HEADS
LANE_DECAY = FOX_HEADS + GDN_HEADS


def _split3_dot(tri, x):
    bf16 = jnp.bfloat16
    hi = x.astype(bf16)
    r1 = x - hi.astype(jnp.float32)
    mid = r1.astype(bf16)
    lo = (r1 - mid.astype(jnp.float32)).astype(bf16)
    dot = lambda t: jnp.dot(tri, t, preferred_element_type=jnp.float32)
    return dot(hi) + dot(mid) + dot(lo)


def _softplus(x):
    return jnp.maximum(x, 0.0) + jnp.log(1.0 + jnp.exp(-jnp.abs(x)))


def _gate_prep_kernel(small_ref, bias_ref, alog_ref, aux_ref, cumt_ref, carry_ref):
    @pl.when(pl.program_id(1) == 0)
    def _():
        carry_ref[...] = jnp.zeros_like(carry_ref)

    x = small_ref[0] + bias_ref[...]
    lane = lax.broadcasted_iota(jnp.int32, x.shape, 1)
    is_fox = lane < LANE_BETA
    is_beta = (lane >= LANE_BETA) & (lane < LANE_DECAY)
    log_f = -_softplus(-x)
    decay = -jnp.exp(alog_ref[...]) * _softplus(x)
    r = lax.broadcasted_iota(jnp.int32, (SEQ_TILE, SEQ_TILE), 0)
    c = lax.broadcasted_iota(jnp.int32, (SEQ_TILE, SEQ_TILE), 1)
    tri = (c <= r).astype(jnp.bfloat16)
    chunk_shift = GDN_CHUNK.bit_length() - 1
    tri_chunk = ((c <= r) & ((r >> chunk_shift) == (c >> chunk_shift))).astype(jnp.bfloat16)
    cum = _split3_dot(tri, jnp.where(is_fox, log_f, 0.0)) + carry_ref[...]
    carry_ref[...] = cum[SEQ_TILE - 1:SEQ_TILE, :]
    gcum = _split3_dot(tri_chunk, jnp.where(is_fox | is_beta, 0.0, decay))
    aux_ref[0] = jnp.where(is_fox, cum, jnp.where(is_beta, jax.nn.sigmoid(x), gcum))
    cumt_ref[0] = cum.T[:SUBLANES, :]


def _gate_prep(small, B, S, fox_fb, gdn_a_log, gdn_dt_bias):
    pad = lambda v, at: jnp.zeros((1, SMALL_WIDTH), jnp.float32).at[0, at:at + v.shape[0]].set(v)
    bias = pad(fox_fb, LANE_FOX) + pad(gdn_dt_bias, LANE_DECAY)
    alog = pad(gdn_a_log, LANE_DECAY)
    return pl.pallas_call(
        _gate_prep_kernel,
        grid=(B, S // SEQ_TILE),
        in_specs=[pl.BlockSpec((1, SEQ_TILE, SMALL_WIDTH), lambda b, s: (b, s, 0)),
                  pl.BlockSpec((1, SMALL_WIDTH), lambda b, s: (0, 0)), pl.BlockSpec((1, SMALL_WIDTH), lambda b, s: (0, 0))],
        out_specs=[pl.BlockSpec((1, SEQ_TILE, SMALL_WIDTH), lambda b, s: (b, s, 0)),
                   pl.BlockSpec((1, SUBLANES, SEQ_TILE), lambda b, s: (b, 0, s))],
        out_shape=[jax.ShapeDtypeStruct((B, S, SMALL_WIDTH), jnp.float32), jax.ShapeDtypeStruct((B, SUBLANES, S), jnp.float32)],
        scratch_shapes=[pltpu.VMEM((1, SMALL_WIDTH), jnp.float32)],
        compiler_params=pltpu.CompilerParams(dimension_semantics=("arbitrary", "arbitrary"), vmem_limit_bytes=VMEM_LIMIT_BYTES),
        name="gate_prep",
    )(small.reshape(B, S, SMALL_WIDTH), bias, alog)


ATT_Q = 256
ATT_K = 1024
HEADS_PER_BLOCK = LANES // FOX_HEAD_DIM


def _fox_kernel(q_ref, k_ref, v_ref, cumt_ref, o_ref):
    bf16 = jnp.bfloat16
    hp = pl.program_id(1)
    qi = pl.program_id(2)
    q = q_ref[0] * jnp.asarray(FOX_HEAD_DIM ** -0.5, bf16)
    lane = lax.broadcasted_iota(jnp.int32, (ATT_Q, LANES), 1)
    sub_k = lax.broadcasted_iota(jnp.int32, (SUBLANES, ATT_K), 0)
    heads = [hp * HEADS_PER_BLOCK + a for a in range(HEADS_PER_BLOCK)]
    in_head = [(lane >= a * FOX_HEAD_DIM) & (lane < (a + 1) * FOX_HEAD_DIM) for a in range(HEADS_PER_BLOCK)]
    q_head = [jnp.where(in_head[a], q, jnp.zeros_like(q)) for a in range(HEADS_PER_BLOCK)]
    n_full = (qi * ATT_Q) // ATT_K
    q_off = qi * ATT_Q - n_full * ATT_K
    visible = (lax.broadcasted_iota(jnp.int32, (ATT_Q, ATT_K), 1)
               <= lax.broadcasted_iota(jnp.int32, (ATT_Q, ATT_K), 0) + q_off)

    def per_lane(vals):
        out = vals[-1]
        for a in range(HEADS_PER_BLOCK - 2, -1, -1):
            out = jnp.where(in_head[a], vals[a], out)
        return out

    def block(j, carry, masked):
        stats, acc = carry
        rows = pl.ds(pl.multiple_of(j * ATT_K, ATT_K), ATT_K)
        k = k_ref[0, rows, :]
        v = v_ref[0, rows, :]
        cum_k = cumt_ref[0, :, rows]
        new_stats, alphas, pvs = [], [], []
        for a in range(HEADS_PER_BLOCK):
            m, l = stats[a]
            s = lax.dot_general(q_head[a], k, (((1,), (1,)), ((), ())), preferred_element_type=jnp.float32)
            s = s - jnp.sum(jnp.where(sub_k == heads[a], cum_k, 0.0), axis=0, keepdims=True)
            if masked:
                s = jnp.where(visible, s, -jnp.inf)
            m_new = jnp.maximum(m, jnp.max(s, axis=1, keepdims=True))
            alpha = jnp.exp(m - m_new)
            p = jnp.exp(s - m_new)
            new_stats.append((m_new, alpha * l + jnp.sum(p, axis=1, keepdims=True)))
            alphas.append(alpha)
            pvs.append(jnp.dot(p.astype(bf16), v, preferred_element_type=jnp.float32))
        return tuple(new_stats), per_lane(alphas) * acc + per_lane(pvs)

    stat0 = (jnp.full((ATT_Q, 1), -jnp.inf, jnp.float32), jnp.zeros((ATT_Q, 1), jnp.float32))
    init = ((stat0,) * HEADS_PER_BLOCK, jnp.zeros((ATT_Q, LANES), jnp.float32))
    carry = lax.fori_loop(0, n_full, functools.partial(block, masked=False), init)
    stats, acc = block(n_full, carry, masked=True)
    o_ref[0] = acc / per_lane([l for _, l in stats])


def _fox_attention(fox_qkv, cumt, B, S):
    qkv = fox_qkv.reshape(B, S, 3 * FOX_WIDTH)
    blocks = FOX_WIDTH // LANES
    out = pl.pallas_call(
        _fox_kernel,
        grid=(B, blocks, S // ATT_Q),
        in_specs=[pl.BlockSpec((1, ATT_Q, LANES), lambda b, hp, qi: (b, qi, hp)),
                  pl.BlockSpec((1, S, LANES), lambda b, hp, qi: (b, 0, blocks + hp)),
                  pl.BlockSpec((1, S, LANES), lambda b, hp, qi: (b, 0, 2 * blocks + hp)),
                  pl.BlockSpec((1, SUBLANES, S), lambda b, hp, qi: (b, 0, 0))],
        out_specs=pl.BlockSpec((1, ATT_Q, LANES), lambda b, hp, qi: (b, qi, hp)),
        out_shape=jax.ShapeDtypeStruct((B, S, FOX_WIDTH), jnp.float32),
        compiler_params=pltpu.CompilerParams(dimension_semantics=("arbitrary",) * 3, vmem_limit_bytes=VMEM_LIMIT_BYTES),
        name="fox_attention",
    )(qkv, qkv, qkv, cumt)
    return out.reshape(B * S, FOX_WIDTH)


POOL_HALO = SUBLANES * 2


def _shift_rows(x, k):
    return pltpu.roll(x, k, 0)


def _pool_kernel(x_ref, w_ref, scale_ref, o_ref, halo_ref):
    @pl.when(pl.program_id(1) == 0)
    def _():
        halo_ref[...] = jnp.zeros_like(halo_ref)

    s_blk = pl.program_id(1)
    rows = POOL_HALO + SEQ_TILE
    row = lax.broadcasted_iota(jnp.int32, (rows, POOL_GROUP_DIM), 0)
    t_abs = s_blk * SEQ_TILE + row - POOL_HALO
    for g, window in enumerate(POOL_WINDOWS):
        cols = slice(g * POOL_GROUP_DIM, (g + 1) * POOL_GROUP_DIM)
        x = x_ref[0, :, cols]
        ext = jnp.concatenate([halo_ref[:, cols], x], axis=0)
        acc = ext
        span = 1
        while span < window:
            acc = acc + jnp.where(row >= span, _shift_rows(acc, span), 0.0)
            span *= 2
        win_sum = acc[POOL_HALO:, :]
        count = jnp.minimum(t_abs[POOL_HALO:, :] + 1, window).astype(jnp.float32)
        pooled = win_sum / count - x
        mixed = jnp.dot(pooled.astype(jnp.bfloat16), w_ref[g], preferred_element_type=jnp.float32)
        o_ref[0, :, cols] = mixed * scale_ref[:, cols]
    halo_ref[...] = x_ref[0, SEQ_TILE - POOL_HALO:, :]


def _multiscale_pool(pool_in, w_pool, pool_scale, B, S):
    out = pl.pallas_call(
        _pool_kernel,
        grid=(B, S // SEQ_TILE),
        in_specs=[pl.BlockSpec((1, SEQ_TILE, POOL_WIDTH), lambda b, s: (b, s, 0)),
                  pl.BlockSpec(w_pool.shape, lambda b, s: (0, 0, 0)), pl.BlockSpec((1, POOL_WIDTH), lambda b, s: (0, 0))],
        out_specs=pl.BlockSpec((1, SEQ_TILE, POOL_WIDTH), lambda b, s: (b, s, 0)),
        out_shape=jax.ShapeDtypeStruct((B, S, POOL_WIDTH), jnp.float32),
        scratch_shapes=[pltpu.VMEM((POOL_HALO, POOL_WIDTH), jnp.float32)],
        compiler_params=pltpu.CompilerParams(dimension_semantics=("arbitrary", "arbitrary"), vmem_limit_bytes=VMEM_LIMIT_BYTES),
        name="pool",
    )(pool_in.reshape(B, S, POOL_WIDTH), w_pool.astype(jnp.bfloat16), pool_scale.reshape(1, POOL_WIDTH))
    return out.reshape(B * S, POOL_WIDTH)


GDN_TILE_CHUNKS = SEQ_TILE // GDN_CHUNK
GDN_HALO = SUBLANES


def _l2norm(x):
    return x * lax.rsqrt(jnp.sum(x * x, axis=-1, keepdims=True) + L2_EPS)


def _silu(x):
    return x * jax.nn.sigmoid(x)


def _bdot(a, b, dims):
    return lax.dot_general(a.astype(jnp.bfloat16), b.astype(jnp.bfloat16), ((dims[0], dims[1]), ((0,), (0,))),
                           preferred_element_type=jnp.float32)


def _bdot3(a, b):
    bf16 = jnp.bfloat16
    a_hi, b_hi = a.astype(bf16), b.astype(bf16)
    a_lo = (a - a_hi.astype(jnp.float32)).astype(bf16)
    b_lo = (b - b_hi.astype(jnp.float32)).astype(bf16)
    dot = lambda x, y: lax.dot_general(x, y, (((2,), (1,)), ((0,), (0,))), preferred_element_type=jnp.float32)
    return dot(a_hi, b_hi) + dot(a_hi, b_lo) + dot(a_lo, b_hi)


def _unit_lower_inverse(low):
    c = low.shape[-1]
    eye = (lax.broadcasted_iota(jnp.int32, (c, c), 0) == lax.broadcasted_iota(jnp.int32, (c, c), 1)).astype(jnp.float32)
    inv = eye[None] - low
    power = low
    span = 2
    while span < c:
        power = _bdot3(power, power)
        inv = inv + _bdot3(inv, power)
        span *= 2
    return inv


def _gdn_kernel(qkv_ref, z_ref, aux_ref, convw_ref, norm_ref, o_ref, halo_ref, state_ref):
    @pl.when(pl.program_id(1) == 0)
    def _():
        halo_ref[...] = jnp.zeros_like(halo_ref)
        state_ref[...] = jnp.zeros_like(state_ref)

    nc, c, d = GDN_TILE_CHUNKS, GDN_CHUNK, GDN_HEAD_DIM
    x = qkv_ref[0]
    ext = jnp.concatenate([halo_ref[...], x], axis=0)
    conv = ext * convw_ref[GDN_CONV - 1:GDN_CONV, :]
    for k in range(1, GDN_CONV):
        conv = conv + pltpu.roll(ext, k, 0) * convw_ref[GDN_CONV - 1 - k:GDN_CONV - k, :]
    act = _silu(conv[GDN_HALO:, :])
    halo_ref[...] = x[SEQ_TILE - GDN_HALO:, :]

    aux = aux_ref[0]
    aux_t = aux.T
    lane = lax.broadcasted_iota(jnp.int32, aux.shape, 1)
    ri = lax.broadcasted_iota(jnp.int32, (c, c), 0)
    ci = lax.broadcasted_iota(jnp.int32, (c, c), 1)
    heads = []
    for hd in range(GDN_HEADS):
        q = _l2norm(act[:, hd * d:(hd + 1) * d]) * (d ** -0.5)
        k = _l2norm(act[:, GDN_WIDTH + hd * d:GDN_WIDTH + (hd + 1) * d])
        v = act[:, 2 * GDN_WIDTH + hd * d:2 * GDN_WIDTH + (hd + 1) * d]
        beta = jnp.sum(jnp.where(lane == LANE_BETA + hd, aux, 0.0), axis=1, keepdims=True)
        g_col = jnp.sum(jnp.where(lane == LANE_DECAY + hd, aux, 0.0), axis=1, keepdims=True)
        g_row = aux_t[LANE_DECAY + hd:LANE_DECAY + hd + 1, :]
        diff = jnp.stack([g_col[i * c:(i + 1) * c, :] - g_row[:, i * c:(i + 1) * c] for i in range(nc)])
        k_beta = k * beta
        q3, k3 = q.reshape(nc, c, d), k.reshape(nc, c, d)
        g3 = g_col.reshape(nc, c, 1)
        g_last = g3[:, c - 1:c, :]
        heads.append(dict(
            low=_bdot(k_beta.reshape(nc, c, d), k3, ((2,), (2,))) * jnp.exp(jnp.where((ri > ci)[None], diff, -jnp.inf)),
            v_beta=(v * beta).reshape(nc, c, d),
            k_beta_dec=(k_beta * jnp.exp(g_col)).reshape(nc, c, d),
            a_intra=_bdot(q3, k3, ((2,), (2,))) * jnp.exp(jnp.where((ri >= ci)[None], diff, -jnp.inf)),
            q_dec=(q * jnp.exp(g_col)).reshape(nc, c, d),
            k_dec=k3 * jnp.exp(g_last - g3),
            g_last=g_last))
    t_inv = _unit_lower_inverse(jnp.concatenate([h["low"] for h in heads], axis=0))
    for hd, h in enumerate(heads):
        t_h = t_inv[hd * nc:(hd + 1) * nc]
        h["u"] = _bdot(t_h, h["v_beta"], ((2,), (1,)))
        h["w"] = _bdot(t_h, h["k_beta_dec"], ((2,), (1,)))
    bf16 = jnp.bfloat16
    states = [state_ref[hd] for hd in range(GDN_HEADS)]
    outs = [[] for _ in range(GDN_HEADS)]
    for i in range(nc):
        for hd, h in enumerate(heads):
            state_b = states[hd].astype(bf16)
            v_new = h["u"][i] - jnp.dot(h["w"][i].astype(bf16), state_b, preferred_element_type=jnp.float32)
            v_new_b = v_new.astype(bf16)
            outs[hd].append(jnp.dot(h["q_dec"][i].astype(bf16), state_b, preferred_element_type=jnp.float32)
                            + jnp.dot(h["a_intra"][i].astype(bf16), v_new_b, preferred_element_type=jnp.float32))
            states[hd] = states[hd] * jnp.exp(h["g_last"][i]) + lax.dot_general(
                h["k_dec"][i].astype(bf16), v_new_b, (((0,), (0,)), ((), ())), preferred_element_type=jnp.float32)
    for hd in range(GDN_HEADS):
        cols = slice(hd * d, (hd + 1) * d)
        state_ref[hd] = states[hd]
        o = jnp.concatenate(outs[hd], axis=0)
        o = o * lax.rsqrt(jnp.mean(o * o, axis=-1, keepdims=True) + RMS_EPS) * norm_ref[...]
        o_ref[0, :, cols] = o * _silu(z_ref[0, :, cols])


def _gated_deltanet(gdn_qkv, gdn_z, aux, conv_w, norm_g, B, S):
    width = 3 * GDN_WIDTH
    out = pl.pallas_call(
        _gdn_kernel,
        grid=(B, S // SEQ_TILE),
        in_specs=[pl.BlockSpec((1, SEQ_TILE, width), lambda b, s: (b, s, 0)),
                  pl.BlockSpec((1, SEQ_TILE, GDN_WIDTH), lambda b, s: (b, s, 0)),
                  pl.BlockSpec((1, SEQ_TILE, SMALL_WIDTH), lambda b, s: (b, s, 0)),
                  pl.BlockSpec((GDN_CONV, width), lambda b, s: (0, 0)),
                  pl.BlockSpec((1, GDN_HEAD_DIM), lambda b, s: (0, 0))],
        out_specs=pl.BlockSpec((1, SEQ_TILE, GDN_WIDTH), lambda b, s: (b, s, 0)),
        out_shape=jax.ShapeDtypeStruct((B, S, GDN_WIDTH), jnp.float32),
        scratch_shapes=[pltpu.VMEM((GDN_HALO, width), jnp.float32),
                        pltpu.VMEM((GDN_HEADS, GDN_HEAD_DIM, GDN_HEAD_DIM), jnp.float32)],
        compiler_params=pltpu.CompilerParams(dimension_semantics=("arbitrary", "arbitrary"), vmem_limit_bytes=VMEM_LIMIT_BYTES),
        name="gated_deltanet",
    )(gdn_qkv.reshape(B, S, width), gdn_z.reshape(B, S, GDN_WIDTH), aux, conv_w.reshape(GDN_CONV, width),
      norm_g.reshape(1, GDN_HEAD_DIM))
    return out.reshape(B * S, GDN_WIDTH)


def _mixer_branches(proj, B, S, fox_fb, pool_w, pool_scale, gdn_conv, gdn_a_log, gdn_dt_bias, gdn_norm):
    aux, cumt = _gate_prep(proj["small"], B, S, fox_fb, gdn_a_log, gdn_dt_bias)
    branch_a = _fox_attention(proj["fox_qkv"], cumt, B, S)
    branch_b = _multiscale_pool(proj["pool_in"], pool_w, pool_scale, B, S)
    branch_c = _gated_deltanet(proj["gdn_qkv"], proj["gdn_z"], aux, gdn_conv, gdn_norm, B, S)
    return branch_a, branch_b, branch_c


def kernel(x, p, w_in, fox_fb, pool_w, pool_scale, gdn_conv, gdn_a_log, gdn_dt_bias, gdn_norm, w_branch, w_out, ln1_g, ln1_b, peer_wq, peer_k1, peer_k2, peer_u, peer_v, ple_gate, ple_proj, ln2_g, ln2_b):
    B, S, D = x.shape
    h2 = x.reshape(B * S, D)
    for i in range(DEPTH):
        proj = _inproj(h2, _rearrange_w_in(w_in[i]))
        br_a, br_b, br_c = _mixer_branches(proj, B, S, fox_fb[i], pool_w[i], pool_scale[i], gdn_conv[i], gdn_a_log[i],
                                           gdn_dt_bias[i], gdn_norm[i])
        h2 = _merge(h2, br_a, br_b, br_c, proj["gates"], w_branch[i], w_out[i], ln1_g[i], ln1_b[i])
        idx_t, wts_t = _peer_select(h2, peer_wq[i], peer_k1[i], peer_k2[i])
        ffn = _peer_retrieve(h2, idx_t, wts_t, _pack_table(peer_u[i]), _pack_table(peer_v[i]))
        h2 = _ple_ln(h2, p[i].reshape(B * S, -1), ffn, ple_gate[i], ple_proj[i], ln2_g[i], ln2_b[i])
    return h2.reshape(B, S, D)
```

```python
import functools

import jax
import jax.numpy as jnp
import numpy as np
from jax import lax
from jax.experimental import pallas as pl
from jax.experimental.pallas import tpu as pltpu

D_MODEL = 1024
DEPTH = 2
FOX_HEADS = 8
FOX_HEAD_DIM = 64
FOX_WIDTH = FOX_HEADS * FOX_HEAD_DIM
POOL_WINDOWS = (2, 4, 8, 16)
POOL_GROUPS = 4
POOL_GROUP_DIM = 128
POOL_WIDTH = POOL_GROUPS * POOL_GROUP_DIM
GDN_HEADS = 4
GDN_HEAD_DIM = 128
GDN_WIDTH = GDN_HEADS * GDN_HEAD_DIM
GDN_CONV = 4
GDN_CHUNK = 64
N_BRANCHES = 3
IN_SPLITS = (3 * FOX_WIDTH, FOX_HEADS, POOL_WIDTH, 3 * GDN_WIDTH, GDN_HEADS, GDN_HEADS, GDN_WIDTH, N_BRANCHES * D_MODEL)
PEER_HEADS = 8
PEER_KEYS = 128
PEER_N_EXPERTS = PEER_KEYS * PEER_KEYS
PEER_QUERY_DIM = 256
PEER_HALF = PEER_QUERY_DIM // 2
PEER_TOPK = 16
PEER_PAIRS = PEER_HEADS * PEER_TOPK
DEEPNORM_ALPHA = (2 * DEPTH) ** 0.25
LN_EPS = 1e-5
RMS_EPS = 1e-6
L2_EPS = 1e-6

LANES = 128
SUBLANES = 8
VMEM_LIMIT_BYTES = 52 * 1024 * 1024

PEER_TOKEN_TILE = 128
PEER_IDX_GROUP = 8
PEER_UNROLL = 2
ROW_WORDS = D_MODEL // 2 // LANES
PLANE_STRIDE = PEER_PAIRS + 1
PLANE_ROWS = (ROW_WORDS - 1) * PLANE_STRIDE + PEER_PAIRS
PACK_ROWS = 512


def _pack_kernel(tab_ref, out_ref):
    def bf16_bits(x):
        return lax.bitcast_convert_type(x.astype(jnp.bfloat16).astype(jnp.float32), jnp.int32)

    for s in range(ROW_WORDS):
        lo = bf16_bits(tab_ref[:, s * LANES:(s + 1) * LANES])
        hi = bf16_bits(tab_ref[:, (s + ROW_WORDS) * LANES:(s + ROW_WORDS + 1) * LANES])
        out_ref[pl.ds(s, PACK_ROWS, stride=ROW_WORDS), :] = lax.shift_right_logical(lo, 16) | hi


def _pack_table(tab):
    e = tab.shape[0]
    return pl.pallas_call(
        _pack_kernel,
        grid=(e // PACK_ROWS,),
        in_specs=[pl.BlockSpec((PACK_ROWS, D_MODEL), lambda i: (i, 0))],
        out_specs=pl.BlockSpec((PACK_ROWS * ROW_WORDS, LANES), lambda i: (i, 0)),
        out_shape=jax.ShapeDtypeStruct((e * ROW_WORDS, LANES), jnp.int32),
        compiler_params=pltpu.CompilerParams(dimension_semantics=("arbitrary",), vmem_limit_bytes=VMEM_LIMIT_BYTES),
        name="pack_table",
    )(tab)


def _unpack_lo(w):
    return lax.bitcast_convert_type(w << 16, jnp.float32)


def _unpack_hi(w):
    return lax.bitcast_convert_type(w & jnp.int32(-65536), jnp.float32)


def _gather_rows(idx_ref, tab_ref, tile_ref, t, offs):
    g = len(offs)
    for k in range(PEER_PAIRS // g):
        group_idx = idx_ref.at[t, pl.ds(k * g, g)]
        for j in range(g):
            row = pl.multiple_of(group_idx[offs[j]], ROW_WORDS)
            tile_ref[pl.ds(k * g + j, ROW_WORDS, stride=PLANE_STRIDE), :] = tab_ref[pl.ds(row, ROW_WORDS), :]


def _token_pairs(idx_ref, offs_ref, tab_ref, tiles, compute, init):
    offs = [offs_ref[j] for j in range(PEER_IDX_GROUP)]
    _gather_rows(idx_ref, tab_ref, tiles[0], 0, offs)

    def group(i, carry):
        for k in range(PEER_UNROLL):
            t = PEER_UNROLL * i + k
            _gather_rows(idx_ref, tab_ref, tiles[(k + 1) % 2], jnp.minimum(t + 1, PEER_TOKEN_TILE - 1), offs)
            carry = compute(t, tiles[k % 2], carry)
        return carry

    return lax.fori_loop(0, PEER_TOKEN_TILE // PEER_UNROLL, group, init)


def _peer_u_kernel(idx_ref, offs_ref, x_ref, wt_ref, tab_ref, ct_ref, tile0_ref, tile1_ref):
    lane = lax.broadcasted_iota(jnp.int32, (PEER_PAIRS, PEER_TOKEN_TILE), 1)

    def lane_sum_into(act_t, partial, t):
        return jnp.where(lane == t, jnp.sum(partial, axis=1, keepdims=True), act_t)

    def token(t, tile_ref, carry):
        act_t, prev = carry
        act_t = lane_sum_into(act_t, prev, t - 1)
        x = x_ref[t]
        acc = jnp.zeros((PEER_PAIRS, LANES), jnp.float32)
        for s in range(ROW_WORDS):
            w = tile_ref[pl.ds(s * PLANE_STRIDE, PEER_PAIRS), :]
            acc = acc + _unpack_lo(w) * x[s:s + 1, :] + _unpack_hi(w) * x[s + ROW_WORDS:s + ROW_WORDS + 1, :]
        return act_t, acc

    act_t, last = _token_pairs(idx_ref, offs_ref, tab_ref, (tile0_ref, tile1_ref), token,
                               (jnp.zeros((PEER_PAIRS, PEER_TOKEN_TILE), jnp.float32),
                                jnp.zeros((PEER_PAIRS, LANES), jnp.float32)))
    act_t = lane_sum_into(act_t, last, PEER_TOKEN_TILE - 1)
    gelu = 0.5 * act_t * (1.0 + lax.erf(act_t * (2.0 ** -0.5)))
    ct_ref[...] = gelu * wt_ref[...]


def _peer_v_kernel(idx_ref, offs_ref, ct_ref, tab_ref, out_ref, tile0_ref, tile1_ref):
    lane = lax.broadcasted_iota(jnp.int32, (PEER_PAIRS, PEER_TOKEN_TILE), 1)

    def token(t, tile_ref, carry):
        c = jnp.sum(jnp.where(lane == t, ct_ref[...], 0.0), axis=1, keepdims=True)
        lo, hi = [], []
        for s in range(ROW_WORDS):
            w = tile_ref[pl.ds(s * PLANE_STRIDE, PEER_PAIRS), :]
            lo.append(jnp.sum(_unpack_lo(w) * c, axis=0, keepdims=True))
            hi.append(jnp.sum(_unpack_hi(w) * c, axis=0, keepdims=True))
        out_ref[t] = jnp.concatenate(lo + hi, axis=0)
        return carry

    _token_pairs(idx_ref, offs_ref, tab_ref, (tile0_ref, tile1_ref), token, 0)


def _resident_spec(shape):
    return pl.BlockSpec(shape, lambda i: (0,) * len(shape), pipeline_mode=pl.Buffered(1))


def _peer_retrieve(x, idx, wts_t, u_packed, v_packed):
    n = x.shape[0]
    grid = (n // PEER_TOKEN_TILE,)
    params = pltpu.CompilerParams(dimension_semantics=("arbitrary",), vmem_limit_bytes=VMEM_LIMIT_BYTES)
    idx_spec = pl.BlockSpec((PEER_TOKEN_TILE, PEER_PAIRS), lambda i: (i, 0), memory_space=pltpu.SMEM)
    offs_spec = pl.BlockSpec((PEER_IDX_GROUP,), lambda i: (0,), memory_space=pltpu.SMEM)
    offs = jnp.arange(PEER_IDX_GROUP, dtype=jnp.int32)
    pair_spec = pl.BlockSpec((PEER_PAIRS, PEER_TOKEN_TILE), lambda i: (0, i))
    row_spec = pl.BlockSpec((PEER_TOKEN_TILE, SUBLANES, LANES), lambda i: (i, 0, 0))
    tiles = [pltpu.VMEM((PLANE_ROWS, LANES), jnp.int32)] * 2
    ct = pl.pallas_call(
        _peer_u_kernel,
        grid=grid,
        in_specs=[idx_spec, offs_spec, row_spec, pair_spec, _resident_spec(u_packed.shape)],
        out_specs=pair_spec,
        out_shape=jax.ShapeDtypeStruct((PEER_PAIRS, n), jnp.float32),
        scratch_shapes=tiles,
        compiler_params=params,
        name="peer_u",
    )(idx, offs, x.reshape(n, SUBLANES, LANES), wts_t, u_packed)
    out = pl.pallas_call(
        _peer_v_kernel,
        grid=grid,
        in_specs=[idx_spec, offs_spec, pair_spec, _resident_spec(v_packed.shape)],
        out_specs=row_spec,
        out_shape=jax.ShapeDtypeStruct((n, SUBLANES, LANES), jnp.float32),
        scratch_shapes=tiles,
        compiler_params=params,
        name="peer_v",
    )(idx, offs, ct, v_packed)
    return out.reshape(n, D_MODEL)


SEL_TILE = 256
SEL_CHUNK = 2 * LANES
_BIG_POS = 1 << 30
CAND_TILES = tuple((a, bh) for a in range(PEER_TOPK) for bh in range(2) if (a + 1) * (bh * SUBLANES + 1) <= PEER_TOPK)


def _allreduce8(x, op):
    for shift in (4, 2, 1):
        x = op(x, pltpu.roll(x, shift, 0))
    return x


def _top16(problems):
    scores = [s3 for s3, _ in problems]
    results = [([], []) for _ in problems]
    for _ in range(PEER_TOPK):
        for i, (_, pos3) in enumerate(problems):
            m = _allreduce8(jnp.max(scores[i], axis=0), jnp.maximum)
            sel = _allreduce8(jnp.min(jnp.where(scores[i] == m[None], pos3, _BIG_POS), axis=0), jnp.minimum)
            scores[i] = jnp.where(pos3 == sel[None], -jnp.inf, scores[i])
            results[i][0].append(m)
            results[i][1].append(sel)
    return results


def _rows_to_tiles(rows, sub):
    tiles = []
    for half in range(2):
        t = rows[half * SUBLANES]
        for k in range(1, SUBLANES):
            t = jnp.where(sub == k, rows[half * SUBLANES + k], t)
        tiles.append(t)
    return tiles


def _peer_select_kernel(h_ref, wqt_ref, k1_ref, k2_ref, idx_ref, wts_ref, qt_ref, idxt_ref):
    bf16 = jnp.bfloat16
    qt_ref[...] = lax.dot_general(wqt_ref[...], h_ref[...].astype(bf16), (((1,), (1,)), ((), ())),
                                  preferred_element_type=jnp.float32)
    n_chunks = SEL_TILE // SEL_CHUNK
    g1 = PEER_KEYS // SUBLANES
    key_pos = (lax.broadcasted_iota(jnp.int32, (g1, SUBLANES, SEL_CHUNK), 0) * SUBLANES
               + lax.broadcasted_iota(jnp.int32, (g1, SUBLANES, SEL_CHUNK), 1))
    sub = lax.broadcasted_iota(jnp.int32, (SUBLANES, SEL_CHUNK), 0)

    def head_chunk(j, carry):
        hd = j // n_chunks
        lanes = pl.ds(pl.multiple_of((j % n_chunks) * SEL_CHUNK, SEL_CHUNK), SEL_CHUNK)
        q_row = pl.multiple_of(hd * PEER_QUERY_DIM, PEER_QUERY_DIM)
        q1 = qt_ref[pl.ds(q_row, PEER_HALF), lanes]
        q2 = qt_ref[pl.ds(q_row + PEER_HALF, PEER_HALF), lanes]
        s1 = jnp.dot(k1_ref[hd], q1.astype(bf16), preferred_element_type=jnp.float32)
        s2 = jnp.dot(k2_ref[hd], q2.astype(bf16), preferred_element_type=jnp.float32)
        (v1, i1), (v2, i2) = _top16([(s1.reshape(g1, SUBLANES, SEL_CHUNK), key_pos),
                                     (s2.reshape(g1, SUBLANES, SEL_CHUNK), key_pos)])
        v2t = _rows_to_tiles(v2, sub)
        i2t = _rows_to_tiles(i2, sub)
        cand = jnp.stack([v1[a] + v2t[bh] for a, bh in CAND_TILES])
        cand_key = jnp.stack([(a * PEER_TOPK + bh * SUBLANES) * PEER_N_EXPERTS + sub * PEER_N_EXPERTS
                              + i1[a] * PEER_KEYS + i2t[bh] for a, bh in CAND_TILES])
        ((top_s, top_key),) = _top16([(cand, cand_key)])
        ts = _rows_to_tiles(top_s, sub)
        te = [key & (PEER_N_EXPERTS - 1) for key in _rows_to_tiles(top_key, sub)]
        e0 = jnp.exp(ts[0] - top_s[0])
        e1 = jnp.exp(ts[1] - top_s[0])
        denom = _allreduce8(e0 + e1, jnp.add)
        row = pl.multiple_of(hd * PEER_TOPK, PEER_TOPK)
        idxt_ref[pl.ds(row, SUBLANES), lanes] = te[0] * ROW_WORDS
        idxt_ref[pl.ds(row + SUBLANES, SUBLANES), lanes] = te[1] * ROW_WORDS
        wts_ref[pl.ds(row, SUBLANES), lanes] = e0 / denom
        wts_ref[pl.ds(row + SUBLANES, SUBLANES), lanes] = e1 / denom
        return carry

    lax.fori_loop(0, PEER_HEADS * n_chunks, head_chunk, 0)
    idx_ref[...] = idxt_ref[...].T


def _const_spec(shape):
    return pl.BlockSpec(shape, lambda i: (0,) * len(shape))


def _peer_select(h2, w_q, sub_k1, sub_k2):
    n = h2.shape[0]
    bf16 = jnp.bfloat16
    pair_spec = pl.BlockSpec((PEER_PAIRS, SEL_TILE), lambda i: (0, i))
    return pl.pallas_call(
        _peer_select_kernel,
        grid=(n // SEL_TILE,),
        in_specs=[pl.BlockSpec((SEL_TILE, D_MODEL), lambda i: (i, 0)),
                  _const_spec((PEER_HEADS * PEER_QUERY_DIM, D_MODEL)),
                  _const_spec(sub_k1.shape), _const_spec(sub_k2.shape)],
        out_specs=[pl.BlockSpec((SEL_TILE, PEER_PAIRS), lambda i: (i, 0)), pair_spec],
        out_shape=[jax.ShapeDtypeStruct((n, PEER_PAIRS), jnp.int32), jax.ShapeDtypeStruct((PEER_PAIRS, n), jnp.float32)],
        scratch_shapes=[pltpu.VMEM((PEER_HEADS * PEER_QUERY_DIM, SEL_TILE), jnp.float32),
                        pltpu.VMEM((PEER_PAIRS, SEL_TILE), jnp.int32)],
        compiler_params=pltpu.CompilerParams(dimension_semantics=("arbitrary",), vmem_limit_bytes=VMEM_LIMIT_BYTES),
        name="peer_select",
    )(h2, w_q.T.astype(bf16), sub_k1.astype(bf16), sub_k2.astype(bf16))


ROW_TILE = 512
SMALL_WIDTH = LANES
PROJ_GROUPS = (("fox_qkv", 3 * FOX_WIDTH), ("pool_in", POOL_WIDTH), ("gdn_qkv", 3 * GDN_WIDTH),
               ("gdn_z", GDN_WIDTH), ("gates", N_BRANCHES * D_MODEL), ("small", SMALL_WIDTH))


PROJ_DTYPES = {"fox_qkv": jnp.bfloat16}


def _dense_params():
    return pltpu.CompilerParams(dimension_semantics=("arbitrary",), vmem_limit_bytes=VMEM_LIMIT_BYTES)


def _rearrange_w_in(w_in):
    offs = np.concatenate([[0], np.cumsum(IN_SPLITS)])
    fox_qkv, fox_f, pool_in, gdn_qkv, gdn_b, gdn_a, gdn_z, gates = [w_in[:, int(offs[i]):int(offs[i + 1])] for i in range(8)]
    small = jnp.concatenate([fox_f, gdn_b, gdn_a], axis=1)
    small = jnp.pad(small, ((0, 0), (0, SMALL_WIDTH - small.shape[1])))
    return jnp.concatenate([fox_qkv, pool_in, gdn_qkv, gdn_z, gates, small], axis=1).astype(jnp.bfloat16)


def _inproj_kernel(h_ref, w_ref, *out_refs):
    x = h_ref[...].astype(jnp.bfloat16)
    off = 0
    for (_, width), o_ref in zip(PROJ_GROUPS, out_refs):
        o_ref[...] = jnp.dot(x, w_ref[:, off:off + width], preferred_element_type=jnp.float32).astype(o_ref.dtype)
        off += width


def _inproj(h2, w_r):
    n = h2.shape[0]
    outs = pl.pallas_call(
        _inproj_kernel,
        grid=(n // ROW_TILE,),
        in_specs=[pl.BlockSpec((ROW_TILE, D_MODEL), lambda i: (i, 0)), _const_spec(w_r.shape)],
        out_specs=[pl.BlockSpec((ROW_TILE, w), lambda i: (i, 0)) for _, w in PROJ_GROUPS],
        out_shape=[jax.ShapeDtypeStruct((n, w), PROJ_DTYPES.get(name, jnp.float32)) for name, w in PROJ_GROUPS],
        compiler_params=_dense_params(),
        name="inproj",
    )(h2, w_r)
    return {name: o for (name, _), o in zip(PROJ_GROUPS, outs)}


def _layer_norm_rows(y, g, b):
    mu = jnp.mean(y, axis=-1, keepdims=True)
    d = y - mu
    var = jnp.mean(d * d, axis=-1, keepdims=True)
    return d * lax.rsqrt(var + LN_EPS) * g + b


def _merge_kernel(h_ref, a_ref, b_ref, c_ref, gates_ref, wb_ref, wo_ref, g_ref, beta_ref, o_ref):
    bf16 = jnp.bfloat16
    merged = jnp.zeros((ROW_TILE, D_MODEL), jnp.float32)
    for n, br_ref in enumerate((a_ref, b_ref, c_ref)):
        to_model = jnp.dot(br_ref[...].astype(bf16), wb_ref[n], preferred_element_type=jnp.float32)
        merged = merged + jax.nn.sigmoid(gates_ref[:, n * D_MODEL:(n + 1) * D_MODEL]) * to_model
    mix = jnp.dot(merged.astype(bf16), wo_ref[...], preferred_element_type=jnp.float32)
    o_ref[...] = _layer_norm_rows(DEEPNORM_ALPHA * h_ref[...] + mix, g_ref[...], beta_ref[...])


def _merge(h2, br_a, br_b, br_c, gates, w_branch, w_out, ln_g, ln_b):
    n = h2.shape[0]
    bf16 = jnp.bfloat16
    row = lambda w: pl.BlockSpec((ROW_TILE, w), lambda i: (i, 0))
    return pl.pallas_call(
        _merge_kernel,
        grid=(n // ROW_TILE,),
        in_specs=[row(D_MODEL), row(FOX_WIDTH), row(POOL_WIDTH), row(GDN_WIDTH), row(N_BRANCHES * D_MODEL),
                  _const_spec(w_branch.shape), _const_spec(w_out.shape), _const_spec((1, D_MODEL)), _const_spec((1, D_MODEL))],
        out_specs=row(D_MODEL),
        out_shape=jax.ShapeDtypeStruct((n, D_MODEL), jnp.float32),
        compiler_params=_dense_params(),
        name="merge_ln1",
    )(h2, br_a, br_b, br_c, gates, w_branch.astype(bf16), w_out.astype(bf16), ln_g.reshape(1, -1), ln_b.reshape(1, -1))


def _ple_kernel(h_ref, p_ref, ffn_ref, wg_ref, wp_ref, g_ref, beta_ref, o_ref):
    bf16 = jnp.bfloat16
    h = h_ref[...]
    gate = jax.nn.sigmoid(jnp.dot(h.astype(bf16), wg_ref[...], preferred_element_type=jnp.float32))
    emb = jnp.dot(p_ref[...].astype(bf16), wp_ref[...], preferred_element_type=jnp.float32)
    o_ref[...] = _layer_norm_rows(DEEPNORM_ALPHA * h + ffn_ref[...] + gate * emb, g_ref[...], beta_ref[...])


def _ple_ln(h2, p2, ffn, w_g, w_p, ln_g, ln_b):
    n = h2.shape[0]
    bf16 = jnp.bfloat16
    row = lambda w: pl.BlockSpec((ROW_TILE, w), lambda i: (i, 0))
    return pl.pallas_call(
        _ple_kernel,
        grid=(n // ROW_TILE,),
        in_specs=[row(D_MODEL), row(p2.shape[1]), row(D_MODEL), _const_spec(w_g.shape), _const_spec(w_p.shape),
                  _const_spec((1, D_MODEL)), _const_spec((1, D_MODEL))],
        out_specs=row(D_MODEL),
        out_shape=jax.ShapeDtypeStruct((n, D_MODEL), jnp.float32),
        compiler_params=_dense_params(),
        name="ple_ln2",
    )(h2, p2, ffn, w_g.astype(bf16), w_p.astype(bf16), ln_g.reshape(1, -1), ln_b.reshape(1, -1))


SEQ_TILE = 512
LANE_FOX = 0
LANE_BETA = FOX_HEADS
LANE_DECAY = FOX_HEADS + GDN_HEADS


def _split3_dot(tri, x):
    bf16 = jnp.bfloat16
    hi = x.astype(bf16)
    r1 = x - hi.astype(jnp.float32)
    mid = r1.astype(bf16)
    lo = (r1 - mid.astype(jnp.float32)).astype(bf16)
    dot = lambda t: jnp.dot(tri, t, preferred_element_type=jnp.float32)
    return dot(hi) + dot(mid) + dot(lo)


def _softplus(x):
    return jnp.maximum(x, 0.0) + jnp.log(1.0 + jnp.exp(-jnp.abs(x)))


def _gate_prep_kernel(small_ref, bias_ref, alog_ref, aux_ref, cumt_ref, carry_ref):
    @pl.when(pl.program_id(1) == 0)
    def _():
        carry_ref[...] = jnp.zeros_like(carry_ref)

    x = small_ref[0] + bias_ref[...]
    lane = lax.broadcasted_iota(jnp.int32, x.shape, 1)
    is_fox = lane < LANE_BETA
    is_beta = (lane >= LANE_BETA) & (lane < LANE_DECAY)
    log_f = -_softplus(-x)
    decay = -jnp.exp(alog_ref[...]) * _softplus(x)
    r = lax.broadcasted_iota(jnp.int32, (SEQ_TILE, SEQ_TILE), 0)
    c = lax.broadcasted_iota(jnp.int32, (SEQ_TILE, SEQ_TILE), 1)
    tri = (c <= r).astype(jnp.bfloat16)
    chunk_shift = GDN_CHUNK.bit_length() - 1
    tri_chunk = ((c <= r) & ((r >> chunk_shift) == (c >> chunk_shift))).astype(jnp.bfloat16)
    cum = _split3_dot(tri, jnp.where(is_fox, log_f, 0.0)) + carry_ref[...]
    carry_ref[...] = cum[SEQ_TILE - 1:SEQ_TILE, :]
    gcum = _split3_dot(tri_chunk, jnp.where(is_fox | is_beta, 0.0, decay))
    aux_ref[0] = jnp.where(is_fox, cum, jnp.where(is_beta, jax.nn.sigmoid(x), gcum))
    cumt_ref[0] = cum.T[:SUBLANES, :]


def _gate_prep(small, B, S, fox_fb, gdn_a_log, gdn_dt_bias):
    pad = lambda v, at: jnp.zeros((1, SMALL_WIDTH), jnp.float32).at[0, at:at + v.shape[0]].set(v)
    bias = pad(fox_fb, LANE_FOX) + pad(gdn_dt_bias, LANE_DECAY)
    alog = pad(gdn_a_log, LANE_DECAY)
    return pl.pallas_call(
        _gate_prep_kernel,
        grid=(B, S // SEQ_TILE),
        in_specs=[pl.BlockSpec((1, SEQ_TILE, SMALL_WIDTH), lambda b, s: (b, s, 0)),
                  pl.BlockSpec((1, SMALL_WIDTH), lambda b, s: (0, 0)), pl.BlockSpec((1, SMALL_WIDTH), lambda b, s: (0, 0))],
        out_specs=[pl.BlockSpec((1, SEQ_TILE, SMALL_WIDTH), lambda b, s: (b, s, 0)),
                   pl.BlockSpec((1, SUBLANES, SEQ_TILE), lambda b, s: (b, 0, s))],
        out_shape=[jax.ShapeDtypeStruct((B, S, SMALL_WIDTH), jnp.float32), jax.ShapeDtypeStruct((B, SUBLANES, S), jnp.float32)],
        scratch_shapes=[pltpu.VMEM((1, SMALL_WIDTH), jnp.float32)],
        compiler_params=pltpu.CompilerParams(dimension_semantics=("arbitrary", "arbitrary"), vmem_limit_bytes=VMEM_LIMIT_BYTES),
        name="gate_prep",
    )(small.reshape(B, S, SMALL_WIDTH), bias, alog)


ATT_Q = 256
ATT_K = 1024
HEADS_PER_BLOCK = LANES // FOX_HEAD_DIM


def _fox_kernel(q_ref, k_ref, v_ref, cumt_ref, o_ref):
    bf16 = jnp.bfloat16
    hp = pl.program_id(1)
    qi = pl.program_id(2)
    q = q_ref[0] * jnp.asarray(FOX_HEAD_DIM ** -0.5, bf16)
    lane = lax.broadcasted_iota(jnp.int32, (ATT_Q, LANES), 1)
    sub_k = lax.broadcasted_iota(jnp.int32, (SUBLANES, ATT_K), 0)
    heads = [hp * HEADS_PER_BLOCK + a for a in range(HEADS_PER_BLOCK)]
    in_head = [(lane >= a * FOX_HEAD_DIM) & (lane < (a + 1) * FOX_HEAD_DIM) for a in range(HEADS_PER_BLOCK)]
    q_head = [jnp.where(in_head[a], q, jnp.zeros_like(q)) for a in range(HEADS_PER_BLOCK)]
    n_full = (qi * ATT_Q) // ATT_K
    q_off = qi * ATT_Q - n_full * ATT_K
    visible = (lax.broadcasted_iota(jnp.int32, (ATT_Q, ATT_K), 1)
               <= lax.broadcasted_iota(jnp.int32, (ATT_Q, ATT_K), 0) + q_off)

    def per_lane(vals):
        out = vals[-1]
        for a in range(HEADS_PER_BLOCK - 2, -1, -1):
            out = jnp.where(in_head[a], vals[a], out)
        return out

    def block(j, carry, masked):
        stats, acc = carry
        rows = pl.ds(pl.multiple_of(j * ATT_K, ATT_K), ATT_K)
        k = k_ref[0, rows, :]
        v = v_ref[0, rows, :]
        cum_k = cumt_ref[0, :, rows]
        scores = [lax.dot_general(q_head[a], k, (((1,), (1,)), ((), ())), preferred_element_type=jnp.float32)
                  for a in range(HEADS_PER_BLOCK)]
        new_stats, alphas, probs = [], [], []
        for a in range(HEADS_PER_BLOCK):
            m, l = stats[a]
            s = scores[a] - jnp.sum(jnp.where(sub_k == heads[a], cum_k, 0.0), axis=0, keepdims=True)
            if masked:
                s = jnp.where(visible, s, -jnp.inf)
            m_new = jnp.maximum(m, jnp.max(s, axis=1, keepdims=True))
            alpha = jnp.exp(m - m_new)
            p = jnp.exp(s - m_new)
            new_stats.append((m_new, alpha * l + jnp.sum(p, axis=1, keepdims=True)))
            alphas.append(alpha)
            probs.append(p.astype(bf16))
        pvs = [jnp.dot(p, v, preferred_element_type=jnp.float32) for p in probs]
        return tuple(new_stats), per_lane(alphas) * acc + per_lane(pvs)

    stat0 = (jnp.full((ATT_Q, 1), -jnp.inf, jnp.float32), jnp.zeros((ATT_Q, 1), jnp.float32))
    init = ((stat0,) * HEADS_PER_BLOCK, jnp.zeros((ATT_Q, LANES), jnp.float32))
    carry = lax.fori_loop(0, n_full, functools.partial(block, masked=False), init)
    stats, acc = block(n_full, carry, masked=True)
    o_ref[0] = acc / per_lane([l for _, l in stats])


def _fox_attention(fox_qkv, cumt, B, S):
    qkv = fox_qkv.reshape(B, S, 3 * FOX_WIDTH)
    blocks = FOX_WIDTH // LANES
    out = pl.pallas_call(
        _fox_kernel,
        grid=(B, blocks, S // ATT_Q),
        in_specs=[pl.BlockSpec((1, ATT_Q, LANES), lambda b, hp, qi: (b, qi, hp)),
                  pl.BlockSpec((1, S, LANES), lambda b, hp, qi: (b, 0, blocks + hp)),
                  pl.BlockSpec((1, S, LANES), lambda b, hp, qi: (b, 0, 2 * blocks + hp)),
                  pl.BlockSpec((1, SUBLANES, S), lambda b, hp, qi: (b, 0, 0))],
        out_specs=pl.BlockSpec((1, ATT_Q, LANES), lambda b, hp, qi: (b, qi, hp)),
        out_shape=jax.ShapeDtypeStruct((B, S, FOX_WIDTH), jnp.float32),
        compiler_params=pltpu.CompilerParams(dimension_semantics=("arbitrary",) * 3, vmem_limit_bytes=VMEM_LIMIT_BYTES),
        name="fox_attention",
    )(qkv, qkv, qkv, cumt)
    return out.reshape(B * S, FOX_WIDTH)


POOL_HALO = SUBLANES * 2


def _shift_rows(x, k):
    return pltpu.roll(x, k, 0)


def _pool_kernel(x_ref, w_ref, scale_ref, o_ref, halo_ref):
    @pl.when(pl.program_id(1) == 0)
    def _():
        halo_ref[...] = jnp.zeros_like(halo_ref)

    s_blk = pl.program_id(1)
    rows = POOL_HALO + SEQ_TILE
    row = lax.broadcasted_iota(jnp.int32, (rows, POOL_GROUP_DIM), 0)
    t_abs = s_blk * SEQ_TILE + row - POOL_HALO
    for g, window in enumerate(POOL_WINDOWS):
        cols = slice(g * POOL_GROUP_DIM, (g + 1) * POOL_GROUP_DIM)
        x = x_ref[0, :, cols]
        ext = jnp.concatenate([halo_ref[:, cols], x], axis=0)
        acc = ext
        span = 1
        while span < window:
            acc = acc + jnp.where(row >= span, _shift_rows(acc, span), 0.0)
            span *= 2
        win_sum = acc[POOL_HALO:, :]
        count = jnp.minimum(t_abs[POOL_HALO:, :] + 1, window).astype(jnp.float32)
        pooled = win_sum / count - x
        mixed = jnp.dot(pooled.astype(jnp.bfloat16), w_ref[g], preferred_element_type=jnp.float32)
        o_ref[0, :, cols] = mixed * scale_ref[:, cols]
    halo_ref[...] = x_ref[0, SEQ_TILE - POOL_HALO:, :]


def _multiscale_pool(pool_in, w_pool, pool_scale, B, S):
    out = pl.pallas_call(
        _pool_kernel,
        grid=(B, S // SEQ_TILE),
        in_specs=[pl.BlockSpec((1, SEQ_TILE, POOL_WIDTH), lambda b, s: (b, s, 0)),
                  pl.BlockSpec(w_pool.shape, lambda b, s: (0, 0, 0)), pl.BlockSpec((1, POOL_WIDTH), lambda b, s: (0, 0))],
        out_specs=pl.BlockSpec((1, SEQ_TILE, POOL_WIDTH), lambda b, s: (b, s, 0)),
        out_shape=jax.ShapeDtypeStruct((B, S, POOL_WIDTH), jnp.float32),
        scratch_shapes=[pltpu.VMEM((POOL_HALO, POOL_WIDTH), jnp.float32)],
        compiler_params=pltpu.CompilerParams(dimension_semantics=("arbitrary", "arbitrary"), vmem_limit_bytes=VMEM_LIMIT_BYTES),
        name="pool",
    )(pool_in.reshape(B, S, POOL_WIDTH), w_pool.astype(jnp.bfloat16), pool_scale.reshape(1, POOL_WIDTH))
    return out.reshape(B * S, POOL_WIDTH)


GDN_TILE_CHUNKS = SEQ_TILE // GDN_CHUNK
GDN_HALO = SUBLANES


def _l2norm(x):
    return x * lax.rsqrt(jnp.sum(x * x, axis=-1, keepdims=True) + L2_EPS)


def _silu(x):
    return x * jax.nn.sigmoid(x)


def _bdot(a, b, dims):
    return lax.dot_general(a.astype(jnp.bfloat16), b.astype(jnp.bfloat16), ((dims[0], dims[1]), ((0,), (0,))),
                           preferred_element_type=jnp.float32)


def _bdot3(a, b):
    bf16 = jnp.bfloat16
    a_hi, b_hi = a.astype(bf16), b.astype(bf16)
    a_lo = (a - a_hi.astype(jnp.float32)).astype(bf16)
    b_lo = (b - b_hi.astype(jnp.float32)).astype(bf16)
    dot = lambda x, y: lax.dot_general(x, y, (((2,), (1,)), ((0,), (0,))), preferred_element_type=jnp.float32)
    return dot(a_hi, b_hi) + dot(a_hi, b_lo) + dot(a_lo, b_hi)


def _unit_lower_inverse(low):
    c = low.shape[-1]
    eye = (lax.broadcasted_iota(jnp.int32, (c, c), 0) == lax.broadcasted_iota(jnp.int32, (c, c), 1)).astype(jnp.float32)
    inv = eye[None] - low
    power = low
    span = 2
    while span < c:
        power = _bdot3(power, power)
        inv = inv + _bdot3(inv, power)
        span *= 2
    return inv


def _gdn_kernel(qkv_ref, z_ref, aux_ref, convw_ref, norm_ref, o_ref, halo_ref, state_ref):
    @pl.when(pl.program_id(1) == 0)
    def _():
        halo_ref[...] = jnp.zeros_like(halo_ref)
        state_ref[...] = jnp.zeros_like(state_ref)

    nc, c, d = GDN_TILE_CHUNKS, GDN_CHUNK, GDN_HEAD_DIM
    x = qkv_ref[0]
    ext = jnp.concatenate([halo_ref[...], x], axis=0)
    conv = ext * convw_ref[GDN_CONV - 1:GDN_CONV, :]
    for k in range(1, GDN_CONV):
        conv = conv + pltpu.roll(ext, k, 0) * convw_ref[GDN_CONV - 1 - k:GDN_CONV - k, :]
    act = _silu(conv[GDN_HALO:, :])
    halo_ref[...] = x[SEQ_TILE - GDN_HALO:, :]

    aux = aux_ref[0]
    aux_t = aux.T
    lane = lax.broadcasted_iota(jnp.int32, aux.shape, 1)
    ri = lax.broadcasted_iota(jnp.int32, (c, c), 0)
    ci = lax.broadcasted_iota(jnp.int32, (c, c), 1)
    heads = []
    for hd in range(GDN_HEADS):
        q = _l2norm(act[:, hd * d:(hd + 1) * d]) * (d ** -0.5)
        k = _l2norm(act[:, GDN_WIDTH + hd * d:GDN_WIDTH + (hd + 1) * d])
        v = act[:, 2 * GDN_WIDTH + hd * d:2 * GDN_WIDTH + (hd + 1) * d]
        beta = jnp.sum(jnp.where(lane == LANE_BETA + hd, aux, 0.0), axis=1, keepdims=True)
        g_col = jnp.sum(jnp.where(lane == LANE_DECAY + hd, aux, 0.0), axis=1, keepdims=True)
        g_row = aux_t[LANE_DECAY + hd:LANE_DECAY + hd + 1, :]
        diff = jnp.stack([g_col[i * c:(i + 1) * c, :] - g_row[:, i * c:(i + 1) * c] for i in range(nc)])
        k_beta = k * beta
        q3, k3 = q.reshape(nc, c, d), k.reshape(nc, c, d)
        g3 = g_col.reshape(nc, c, 1)
        g_last = g3[:, c - 1:c, :]
        heads.append(dict(
            low=_bdot(k_beta.reshape(nc, c, d), k3, ((2,), (2,))) * jnp.exp(jnp.where((ri > ci)[None], diff, -jnp.inf)),
            v_beta=(v * beta).reshape(nc, c, d),
            k_beta_dec=(k_beta * jnp.exp(g_col)).reshape(nc, c, d),
            a_intra=_bdot(q3, k3, ((2,), (2,))) * jnp.exp(jnp.where((ri >= ci)[None], diff, -jnp.inf)),
            q_dec=(q * jnp.exp(g_col)).reshape(nc, c, d),
            k_dec=k3 * jnp.exp(g_last - g3),
            g_last=g_last))
    t_inv = _unit_lower_inverse(jnp.concatenate([h["low"] for h in heads], axis=0))
    for hd, h in enumerate(heads):
        t_h = t_inv[hd * nc:(hd + 1) * nc]
        h["u"] = _bdot(t_h, h["v_beta"], ((2,), (1,)))
        h["w"] = _bdot(t_h, h["k_beta_dec"], ((2,), (1,)))
    bf16 = jnp.bfloat16
    states = [state_ref[hd] for hd in range(GDN_HEADS)]
    outs = [[] for _ in range(GDN_HEADS)]
    for i in range(nc):
        for hd, h in enumerate(heads):
            state_b = states[hd].astype(bf16)
            v_new = h["u"][i] - jnp.dot(h["w"][i].astype(bf16), state_b, preferred_element_type=jnp.float32)
            v_new_b = v_new.astype(bf16)
            outs[hd].append(jnp.dot(h["q_dec"][i].astype(bf16), state_b, preferred_element_type=jnp.float32)
                            + jnp.dot(h["a_intra"][i].astype(bf16), v_new_b, preferred_element_type=jnp.float32))
            states[hd] = states[hd] * jnp.exp(h["g_last"][i]) + lax.dot_general(
                h["k_dec"][i].astype(bf16), v_new_b, (((0,), (0,)), ((), ())), preferred_element_type=jnp.float32)
    for hd in range(GDN_HEADS):
        cols = slice(hd * d, (hd + 1) * d)
        state_ref[hd] = states[hd]
        o = jnp.concatenate(outs[hd], axis=0)
        o = o * lax.rsqrt(jnp.mean(o * o, axis=-1, keepdims=True) + RMS_EPS) * norm_ref[...]
        o_ref[0, :, cols] = o * _silu(z_ref[0, :, cols])


def _gated_deltanet(gdn_qkv, gdn_z, aux, conv_w, norm_g, B, S):
    width = 3 * GDN_WIDTH
    out = pl.pallas_call(
        _gdn_kernel,
        grid=(B, S // SEQ_TILE),
        in_specs=[pl.BlockSpec((1, SEQ_TILE, width), lambda b, s: (b, s, 0)),
                  pl.BlockSpec((1, SEQ_TILE, GDN_WIDTH), lambda b, s: (b, s, 0)),
                  pl.BlockSpec((1, SEQ_TILE, SMALL_WIDTH), lambda b, s: (b, s, 0)),
                  pl.BlockSpec((GDN_CONV, width), lambda b, s: (0, 0)),
                  pl.BlockSpec((1, GDN_HEAD_DIM), lambda b, s: (0, 0))],
        out_specs=pl.BlockSpec((1, SEQ_TILE, GDN_WIDTH), lambda b, s: (b, s, 0)),
        out_shape=jax.ShapeDtypeStruct((B, S, GDN_WIDTH), jnp.float32),
        scratch_shapes=[pltpu.VMEM((GDN_HALO, width), jnp.float32),
                        pltpu.VMEM((GDN_HEADS, GDN_HEAD_DIM, GDN_HEAD_DIM), jnp.float32)],
        compiler_params=pltpu.CompilerParams(dimension_semantics=("arbitrary", "arbitrary"), vmem_limit_bytes=VMEM_LIMIT_BYTES),
        name="gated_deltanet",
    )(gdn_qkv.reshape(B, S, width), gdn_z.reshape(B, S, GDN_WIDTH), aux, conv_w.reshape(GDN_CONV, width),
      norm_g.reshape(1, GDN_HEAD_DIM))
    return out.reshape(B * S, GDN_WIDTH)


def _mixer_branches(proj, B, S, fox_fb, pool_w, pool_scale, gdn_conv, gdn_a_log, gdn_dt_bias, gdn_norm):
    aux, cumt = _gate_prep(proj["small"], B, S, fox_fb, gdn_a_log, gdn_dt_bias)
    branch_a = _fox_attention(proj["fox_qkv"], cumt, B, S)
    branch_b = _multiscale_pool(proj["pool_in"], pool_w, pool_scale, B, S)
    branch_c = _gated_deltanet(proj["gdn_qkv"], proj["gdn_z"], aux, gdn_conv, gdn_norm, B, S)
    return branch_a, branch_b, branch_c


def kernel(x, p, w_in, fox_fb, pool_w, pool_scale, gdn_conv, gdn_a_log, gdn_dt_bias, gdn_norm, w_branch, w_out, ln1_g, ln1_b, peer_wq, peer_k1, peer_k2, peer_u, peer_v, ple_gate, ple_proj, ln2_g, ln2_b):
    B, S, D = x.shape
    h2 = x.reshape(B * S, D)
    for i in range(DEPTH):
        proj = _inproj(h2, _rearrange_w_in(w_in[i]))
        br_a, br_b, br_c = _mixer_branches(proj, B, S, fox_fb[i], pool_w[i], pool_scale[i], gdn_conv[i], gdn_a_log[i],
                                           gdn_dt_bias[i], gdn_norm[i])
        h2 = _merge(h2, br_a, br_b, br_c, proj["gates"], w_branch[i], w_out[i], ln1_g[i], ln1_b[i])
        idx_t, wts_t = _peer_select(h2, peer_wq[i], peer_k1[i], peer_k2[i])
        ffn = _peer_retrieve(h2, idx_t, wts_t, _pack_table(peer_u[i]), _pack_table(peer_v[i]))
        h2 = _ple_ln(h2, p[i].reshape(B * S, -1), ffn, ple_gate[i], ple_proj[i], ln2_g[i], ln2_b[i])
    return h2.reshape(B, S, D)
```

```python
import functools

import jax
import jax.numpy as jnp
import numpy as np
from jax import lax
from jax.experimental import pallas as pl
from jax.experimental.pallas import tpu as pltpu

D_MODEL = 1024
DEPTH = 2
FOX_HEADS = 8
FOX_HEAD_DIM = 64
FOX_WIDTH = FOX_HEADS * FOX_HEAD_DIM
POOL_WINDOWS = (2, 4, 8, 16)
POOL_GROUPS = 4
POOL_GROUP_DIM = 128
POOL_WIDTH = POOL_GROUPS * POOL_GROUP_DIM
GDN_HEADS = 4
GDN_HEAD_DIM = 128
GDN_WIDTH = GDN_HEADS * GDN_HEAD_DIM
GDN_CONV = 4
GDN_CHUNK = 64
N_BRANCHES = 3
IN_SPLITS = (3 * FOX_WIDTH, FOX_HEADS, POOL_WIDTH, 3 * GDN_WIDTH, GDN_HEADS, GDN_HEADS, GDN_WIDTH, N_BRANCHES * D_MODEL)
PEER_HEADS = 8
PEER_KEYS = 128
PEER_N_EXPERTS = PEER_KEYS * PEER_KEYS
PEER_QUERY_DIM = 256
PEER_HALF = PEER_QUERY_DIM // 2
PEER_TOPK = 16
PEER_PAIRS = PEER_HEADS * PEER_TOPK
DEEPNORM_ALPHA = (2 * DEPTH) ** 0.25
LN_EPS = 1e-5
RMS_EPS = 1e-6
L2_EPS = 1e-6

LANES = 128
SUBLANES = 8
VMEM_LIMIT_BYTES = 52 * 1024 * 1024

PEER_TOKEN_TILE = 128
PEER_IDX_GROUP = 8
PEER_UNROLL = 2
ROW_WORDS = D_MODEL // 2 // LANES
PLANE_STRIDE = PEER_PAIRS + 1
PLANE_ROWS = (ROW_WORDS - 1) * PLANE_STRIDE + PEER_PAIRS
PACK_ROWS = 512


def _pack_kernel(tab_ref, out_ref):
    def bf16_bits(x):
        return lax.bitcast_convert_type(x.astype(jnp.bfloat16).astype(jnp.float32), jnp.int32)

    for s in range(ROW_WORDS):
        lo = bf16_bits(tab_ref[:, s * LANES:(s + 1) * LANES])
        hi = bf16_bits(tab_ref[:, (s + ROW_WORDS) * LANES:(s + ROW_WORDS + 1) * LANES])
        out_ref[pl.ds(s, PACK_ROWS, stride=ROW_WORDS), :] = lax.shift_right_logical(lo, 16) | hi


def _pack_table(tab):
    e = tab.shape[0]
    return pl.pallas_call(
        _pack_kernel,
        grid=(e // PACK_ROWS,),
        in_specs=[pl.BlockSpec((PACK_ROWS, D_MODEL), lambda i: (i, 0))],
        out_specs=pl.BlockSpec((PACK_ROWS * ROW_WORDS, LANES), lambda i: (i, 0)),
        out_shape=jax.ShapeDtypeStruct((e * ROW_WORDS, LANES), jnp.int32),
        compiler_params=pltpu.CompilerParams(dimension_semantics=("arbitrary",), vmem_limit_bytes=VMEM_LIMIT_BYTES),
        name="pack_table",
    )(tab)


def _unpack_lo(w):
    return lax.bitcast_convert_type(w << 16, jnp.float32)


def _unpack_hi(w):
    return lax.bitcast_convert_type(w & jnp.int32(-65536), jnp.float32)


def _gather_rows(idx_ref, tab_ref, tile_ref, t, offs):
    g = len(offs)
    for k in range(PEER_PAIRS // g):
        group_idx = idx_ref.at[t, pl.ds(k * g, g)]
        for j in range(g):
            row = pl.multiple_of(group_idx[offs[j]], ROW_WORDS)
            tile_ref[pl.ds(k * g + j, ROW_WORDS, stride=PLANE_STRIDE), :] = tab_ref[pl.ds(row, ROW_WORDS), :]


def _token_pairs(idx_ref, offs_ref, tab_ref, tiles, compute, init):
    offs = [offs_ref[j] for j in range(PEER_IDX_GROUP)]
    _gather_rows(idx_ref, tab_ref, tiles[0], 0, offs)

    def group(i, carry):
        for k in range(PEER_UNROLL):
            t = PEER_UNROLL * i + k
            _gather_rows(idx_ref, tab_ref, tiles[(k + 1) % 2], jnp.minimum(t + 1, PEER_TOKEN_TILE - 1), offs)
            carry = compute(t, tiles[k % 2], carry)
        return carry

    return lax.fori_loop(0, PEER_TOKEN_TILE // PEER_UNROLL, group, init)


def _peer_u_kernel(idx_ref, offs_ref, x_ref, wt_ref, tab_ref, ct_ref, tile0_ref, tile1_ref):
    lane = lax.broadcasted_iota(jnp.int32, (PEER_PAIRS, PEER_TOKEN_TILE), 1)

    def lane_sum_into(act_t, partial, t):
        return jnp.where(lane == t, jnp.sum(partial, axis=1, keepdims=True), act_t)

    def token(t, tile_ref, carry):
        act_t, prev = carry
        act_t = lane_sum_into(act_t, prev, t - 1)
        x = x_ref[t]
        acc = jnp.zeros((PEER_PAIRS, LANES), jnp.float32)
        for s in range(ROW_WORDS):
            w = tile_ref[pl.ds(s * PLANE_STRIDE, PEER_PAIRS), :]
            acc = acc + _unpack_lo(w) * x[s:s + 1, :] + _unpack_hi(w) * x[s + ROW_WORDS:s + ROW_WORDS + 1, :]
        return act_t, acc

    act_t, last = _token_pairs(idx_ref, offs_ref, tab_ref, (tile0_ref, tile1_ref), token,
                               (jnp.zeros((PEER_PAIRS, PEER_TOKEN_TILE), jnp.float32),
                                jnp.zeros((PEER_PAIRS, LANES), jnp.float32)))
    act_t = lane_sum_into(act_t, last, PEER_TOKEN_TILE - 1)
    gelu = 0.5 * act_t * (1.0 + lax.erf(act_t * (2.0 ** -0.5)))
    ct_ref[...] = gelu * wt_ref[...]


def _peer_v_kernel(idx_ref, offs_ref, ct_ref, tab_ref, out_ref, tile0_ref, tile1_ref):
    lane = lax.broadcasted_iota(jnp.int32, (PEER_PAIRS, PEER_TOKEN_TILE), 1)

    def token(t, tile_ref, carry):
        c = jnp.sum(jnp.where(lane == t, ct_ref[...], 0.0), axis=1, keepdims=True)
        lo, hi = [], []
        for s in range(ROW_WORDS):
            w = tile_ref[pl.ds(s * PLANE_STRIDE, PEER_PAIRS), :]
            lo.append(jnp.sum(_unpack_lo(w) * c, axis=0, keepdims=True))
            hi.append(jnp.sum(_unpack_hi(w) * c, axis=0, keepdims=True))
        out_ref[t] = jnp.concatenate(lo + hi, axis=0)
        return carry

    _token_pairs(idx_ref, offs_ref, tab_ref, (tile0_ref, tile1_ref), token, 0)


def _resident_spec(shape):
    return pl.BlockSpec(shape, lambda i: (0,) * len(shape), pipeline_mode=pl.Buffered(1))


def _peer_retrieve(x, idx, wts_t, u_packed, v_packed):
    n = x.shape[0]
    grid = (n // PEER_TOKEN_TILE,)
    params = pltpu.CompilerParams(dimension_semantics=("arbitrary",), vmem_limit_bytes=VMEM_LIMIT_BYTES)
    idx_spec = pl.BlockSpec((PEER_TOKEN_TILE, PEER_PAIRS), lambda i: (i, 0), memory_space=pltpu.SMEM)
    offs_spec = pl.BlockSpec((PEER_IDX_GROUP,), lambda i: (0,), memory_space=pltpu.SMEM)
    offs = jnp.arange(PEER_IDX_GROUP, dtype=jnp.int32)
    pair_spec = pl.BlockSpec((PEER_PAIRS, PEER_TOKEN_TILE), lambda i: (0, i))
    row_spec = pl.BlockSpec((PEER_TOKEN_TILE, SUBLANES, LANES), lambda i: (i, 0, 0))
    tiles = [pltpu.VMEM((PLANE_ROWS, LANES), jnp.int32)] * 2
    ct = pl.pallas_call(
        _peer_u_kernel,
        grid=grid,
        in_specs=[idx_spec, offs_spec, row_spec, pair_spec, _resident_spec(u_packed.shape)],
        out_specs=pair_spec,
        out_shape=jax.ShapeDtypeStruct((PEER_PAIRS, n), jnp.float32),
        scratch_shapes=tiles,
        compiler_params=params,
        name="peer_u",
    )(idx, offs, x.reshape(n, SUBLANES, LANES), wts_t, u_packed)
    out = pl.pallas_call(
        _peer_v_kernel,
        grid=grid,
        in_specs=[idx_spec, offs_spec, pair_spec, _resident_spec(v_packed.shape)],
        out_specs=row_spec,
        out_shape=jax.ShapeDtypeStruct((n, SUBLANES, LANES), jnp.float32),
        scratch_shapes=tiles,
        compiler_params=params,
        name="peer_v",
    )(idx, offs, ct, v_packed)
    return out.reshape(n, D_MODEL)


SEL_TILE = 256
SEL_HEADS = 2
_BIG_POS = 1 << 30
CAND_TILES = tuple((a, bh) for a in range(PEER_TOPK) for bh in range(2) if (a + 1) * (bh * SUBLANES + 1) <= PEER_TOPK)


def _allreduce8(x, op):
    for shift in (4, 2, 1):
        x = op(x, pltpu.roll(x, shift, 0))
    return x


def _top16(problems):
    scores = [s3 for s3, _ in problems]
    results = [([], []) for _ in problems]
    for _ in range(PEER_TOPK):
        for i, (_, pos3) in enumerate(problems):
            m = _allreduce8(jnp.max(scores[i], axis=0), jnp.maximum)
            sel = _allreduce8(jnp.min(jnp.where(scores[i] == m[None], pos3, _BIG_POS), axis=0), jnp.minimum)
            scores[i] = jnp.where(pos3 == sel[None], -jnp.inf, scores[i])
            results[i][0].append(m)
            results[i][1].append(sel)
    return results


def _rows_to_tiles(rows, sub):
    tiles = []
    for half in range(2):
        t = rows[half * SUBLANES]
        for k in range(1, SUBLANES):
            t = jnp.where(sub == k, rows[half * SUBLANES + k], t)
        tiles.append(t)
    return tiles


def _peer_select_kernel(h_ref, wqt_ref, k1_ref, k2_ref, idx_ref, wts_ref, qt_ref, idxt_ref):
    bf16 = jnp.bfloat16
    qt_ref[...] = lax.dot_general(wqt_ref[...], h_ref[...].astype(bf16), (((1,), (1,)), ((), ())),
                                  preferred_element_type=jnp.float32)
    g1 = PEER_KEYS // SUBLANES
    key_pos = (lax.broadcasted_iota(jnp.int32, (g1, SUBLANES, SEL_TILE), 0) * SUBLANES
               + lax.broadcasted_iota(jnp.int32, (g1, SUBLANES, SEL_TILE), 1))
    sub = lax.broadcasted_iota(jnp.int32, (SUBLANES, SEL_TILE), 0)

    def head_group(j, carry):
        hds = [j * SEL_HEADS + u for u in range(SEL_HEADS)]
        stage1 = []
        for hd in hds:
            q_row = pl.multiple_of(hd * PEER_QUERY_DIM, PEER_QUERY_DIM)
            q1 = qt_ref[pl.ds(q_row, PEER_HALF), :]
            q2 = qt_ref[pl.ds(q_row + PEER_HALF, PEER_HALF), :]
            s1 = jnp.dot(k1_ref[hd], q1.astype(bf16), preferred_element_type=jnp.float32)
            s2 = jnp.dot(k2_ref[hd], q2.astype(bf16), preferred_element_type=jnp.float32)
            stage1 += [(s1.reshape(g1, SUBLANES, SEL_TILE), key_pos), (s2.reshape(g1, SUBLANES, SEL_TILE), key_pos)]
        picked = _top16(stage1)
        stage2 = []
        for u in range(SEL_HEADS):
            (v1, i1), (v2, i2) = picked[2 * u], picked[2 * u + 1]
            v2t = _rows_to_tiles(v2, sub)
            i2t = _rows_to_tiles(i2, sub)
            cand = jnp.stack([v1[a] + v2t[bh] for a, bh in CAND_TILES])
            cand_key = jnp.stack([(a * PEER_TOPK + bh * SUBLANES) * PEER_N_EXPERTS + sub * PEER_N_EXPERTS
                                  + i1[a] * PEER_KEYS + i2t[bh] for a, bh in CAND_TILES])
            stage2.append((cand, cand_key))
        for hd, (top_s, top_key) in zip(hds, _top16(stage2)):
            ts = _rows_to_tiles(top_s, sub)
            te = [key & (PEER_N_EXPERTS - 1) for key in _rows_to_tiles(top_key, sub)]
            e0 = jnp.exp(ts[0] - top_s[0])
            e1 = jnp.exp(ts[1] - top_s[0])
            denom = _allreduce8(e0 + e1, jnp.add)
            row = pl.multiple_of(hd * PEER_TOPK, PEER_TOPK)
            idxt_ref[pl.ds(row, SUBLANES), :] = te[0] * ROW_WORDS
            idxt_ref[pl.ds(row + SUBLANES, SUBLANES), :] = te[1] * ROW_WORDS
            wts_ref[pl.ds(row, SUBLANES), :] = e0 / denom
            wts_ref[pl.ds(row + SUBLANES, SUBLANES), :] = e1 / denom
        return carry

    lax.fori_loop(0, PEER_HEADS // SEL_HEADS, head_group, 0)
    idx_ref[...] = idxt_ref[...].T


def _const_spec(shape):
    return pl.BlockSpec(shape, lambda i: (0,) * len(shape))


def _peer_select(h2, w_q, sub_k1, sub_k2):
    n = h2.shape[0]
    bf16 = jnp.bfloat16
    pair_spec = pl.BlockSpec((PEER_PAIRS, SEL_TILE), lambda i: (0, i))
    return pl.pallas_call(
        _peer_select_kernel,
        grid=(n // SEL_TILE,),
        in_specs=[pl.BlockSpec((SEL_TILE, D_MODEL), lambda i: (i, 0)),
                  _const_spec((PEER_HEADS * PEER_QUERY_DIM, D_MODEL)),
                  _const_spec(sub_k1.shape), _const_spec(sub_k2.shape)],
        out_specs=[pl.BlockSpec((SEL_TILE, PEER_PAIRS), lambda i: (i, 0)), pair_spec],
        out_shape=[jax.ShapeDtypeStruct((n, PEER_PAIRS), jnp.int32), jax.ShapeDtypeStruct((PEER_PAIRS, n), jnp.float32)],
        scratch_shapes=[pltpu.VMEM((PEER_HEADS * PEER_QUERY_DIM, SEL_TILE), jnp.float32),
                        pltpu.VMEM((PEER_PAIRS, SEL_TILE), jnp.int32)],
        compiler_params=pltpu.CompilerParams(dimension_semantics=("arbitrary",), vmem_limit_bytes=VMEM_LIMIT_BYTES),
        name="peer_select",
    )(h2, w_q.T.astype(bf16), sub_k1.astype(bf16), sub_k2.astype(bf16))


ROW_TILE = 512
SMALL_WIDTH = LANES
PROJ_GROUPS = (("fox_qkv", 3 * FOX_WIDTH), ("pool_in", POOL_WIDTH), ("gdn_qkv", 3 * GDN_WIDTH),
               ("gdn_z", GDN_WIDTH), ("gates", N_BRANCHES * D_MODEL), ("small", SMALL_WIDTH))


PROJ_DTYPES = {"fox_qkv": jnp.bfloat16}


def _dense_params():
    return pltpu.CompilerParams(dimension_semantics=("arbitrary",), vmem_limit_bytes=VMEM_LIMIT_BYTES)


def _rearrange_w_in(w_in):
    offs = np.concatenate([[0], np.cumsum(IN_SPLITS)])
    fox_qkv, fox_f, pool_in, gdn_qkv, gdn_b, gdn_a, gdn_z, gates = [w_in[:, int(offs[i]):int(offs[i + 1])] for i in range(8)]
    small = jnp.concatenate([fox_f, gdn_b, gdn_a], axis=1)
    small = jnp.pad(small, ((0, 0), (0, SMALL_WIDTH - small.shape[1])))
    return jnp.concatenate([fox_qkv, pool_in, gdn_qkv, gdn_z, gates, small], axis=1).astype(jnp.bfloat16)


def _inproj_kernel(h_ref, w_ref, *out_refs):
    x = h_ref[...].astype(jnp.bfloat16)
    off = 0
    for (_, width), o_ref in zip(PROJ_GROUPS, out_refs):
        o_ref[...] = jnp.dot(x, w_ref[:, off:off + width], preferred_element_type=jnp.float32).astype(o_ref.dtype)
        off += width


def _inproj(h2, w_r):
    n = h2.shape[0]
    outs = pl.pallas_call(
        _inproj_kernel,
        grid=(n // ROW_TILE,),
        in_specs=[pl.BlockSpec((ROW_TILE, D_MODEL), lambda i: (i, 0)), _const_spec(w_r.shape)],
        out_specs=[pl.BlockSpec((ROW_TILE, w), lambda i: (i, 0)) for _, w in PROJ_GROUPS],
        out_shape=[jax.ShapeDtypeStruct((n, w), PROJ_DTYPES.get(name, jnp.float32)) for name, w in PROJ_GROUPS],
        compiler_params=_dense_params(),
        name="inproj",
    )(h2, w_r)
    return {name: o for (name, _), o in zip(PROJ_GROUPS, outs)}


def _layer_norm_rows(y, g, b):
    mu = jnp.mean(y, axis=-1, keepdims=True)
    d = y - mu
    var = jnp.mean(d * d, axis=-1, keepdims=True)
    return d * lax.rsqrt(var + LN_EPS) * g + b


def _merge_kernel(h_ref, a_ref, b_ref, c_ref, gates_ref, wb_ref, wo_ref, g_ref, beta_ref, o_ref):
    bf16 = jnp.bfloat16
    merged = jnp.zeros((ROW_TILE, D_MODEL), jnp.float32)
    for n, br_ref in enumerate((a_ref, b_ref, c_ref)):
        to_model = jnp.dot(br_ref[...].astype(bf16), wb_ref[n], preferred_element_type=jnp.float32)
        merged = merged + jax.nn.sigmoid(gates_ref[:, n * D_MODEL:(n + 1) * D_MODEL]) * to_model
    mix = jnp.dot(merged.astype(bf16), wo_ref[...], preferred_element_type=jnp.float32)
    o_ref[...] = _layer_norm_rows(DEEPNORM_ALPHA * h_ref[...] + mix, g_ref[...], beta_ref[...])


def _merge(h2, br_a, br_b, br_c, gates, w_branch, w_out, ln_g, ln_b):
    n = h2.shape[0]
    bf16 = jnp.bfloat16
    row = lambda w: pl.BlockSpec((ROW_TILE, w), lambda i: (i, 0))
    return pl.pallas_call(
        _merge_kernel,
        grid=(n // ROW_TILE,),
        in_specs=[row(D_MODEL), row(FOX_WIDTH), row(POOL_WIDTH), row(GDN_WIDTH), row(N_BRANCHES * D_MODEL),
                  _const_spec(w_branch.shape), _const_spec(w_out.shape), _const_spec((1, D_MODEL)), _const_spec((1, D_MODEL))],
        out_specs=row(D_MODEL),
        out_shape=jax.ShapeDtypeStruct((n, D_MODEL), jnp.float32),
        compiler_params=_dense_params(),
        name="merge_ln1",
    )(h2, br_a, br_b, br_c, gates, w_branch.astype(bf16), w_out.astype(bf16), ln_g.reshape(1, -1), ln_b.reshape(1, -1))


def _ple_kernel(h_ref, p_ref, ffn_ref, wg_ref, wp_ref, g_ref, beta_ref, o_ref):
    bf16 = jnp.bfloat16
    h = h_ref[...]
    gate = jax.nn.sigmoid(jnp.dot(h.astype(bf16), wg_ref[...], preferred_element_type=jnp.float32))
    emb = jnp.dot(p_ref[...].astype(bf16), wp_ref[...], preferred_element_type=jnp.float32)
    o_ref[...] = _layer_norm_rows(DEEPNORM_ALPHA * h + ffn_ref[...] + gate * emb, g_ref[...], beta_ref[...])


def _ple_ln(h2, p2, ffn, w_g, w_p, ln_g, ln_b):
    n = h2.shape[0]
    bf16 = jnp.bfloat16
    row = lambda w: pl.BlockSpec((ROW_TILE, w), lambda i: (i, 0))
    return pl.pallas_call(
        _ple_kernel,
        grid=(n // ROW_TILE,),
        in_specs=[row(D_MODEL), row(p2.shape[1]), row(D_MODEL), _const_spec(w_g.shape), _const_spec(w_p.shape),
                  _const_spec((1, D_MODEL)), _const_spec((1, D_MODEL))],
        out_specs=row(D_MODEL),
        out_shape=jax.ShapeDtypeStruct((n, D_MODEL), jnp.float32),
        compiler_params=_dense_params(),
        name="ple_ln2",
    )(h2, p2, ffn, w_g.astype(bf16), w_p.astype(bf16), ln_g.reshape(1, -1), ln_b.reshape(1, -1))


SEQ_TILE = 512
LANE_FOX = 0
LANE_BETA = FOX_HEADS
LANE_DECAY = FOX_HEADS + GDN_HEADS


def _split3_dot(tri, x):
    bf16 = jnp.bfloat16
    hi = x.astype(bf16)
    r1 = x - hi.astype(jnp.float32)
    mid = r1.astype(bf16)
    lo = (r1 - mid.astype(jnp.float32)).astype(bf16)
    dot = lambda t: jnp.dot(tri, t, preferred_element_type=jnp.float32)
    return dot(hi) + dot(mid) + dot(lo)


def _softplus(x):
    return jnp.maximum(x, 0.0) + jnp.log(1.0 + jnp.exp(-jnp.abs(x)))


def _gate_prep_kernel(small_ref, bias_ref, alog_ref, aux_ref, cumt_ref, carry_ref):
    @pl.when(pl.program_id(1) == 0)
    def _():
        carry_ref[...] = jnp.zeros_like(carry_ref)

    x = small_ref[0] + bias_ref[...]
    lane = lax.broadcasted_iota(jnp.int32, x.shape, 1)
    is_fox = lane < LANE_BETA
    is_beta = (lane >= LANE_BETA) & (lane < LANE_DECAY)
    log_f = -_softplus(-x)
    decay = -jnp.exp(alog_ref[...]) * _softplus(x)
    r = lax.broadcasted_iota(jnp.int32, (SEQ_TILE, SEQ_TILE), 0)
    c = lax.broadcasted_iota(jnp.int32, (SEQ_TILE, SEQ_TILE), 1)
    tri = (c <= r).astype(jnp.bfloat16)
    chunk_shift = GDN_CHUNK.bit_length() - 1
    tri_chunk = ((c <= r) & ((r >> chunk_shift) == (c >> chunk_shift))).astype(jnp.bfloat16)
    cum = _split3_dot(tri, jnp.where(is_fox, log_f, 0.0)) + carry_ref[...]
    carry_ref[...] = cum[SEQ_TILE - 1:SEQ_TILE, :]
    gcum = _split3_dot(tri_chunk, jnp.where(is_fox | is_beta, 0.0, decay))
    aux_ref[0] = jnp.where(is_fox, cum, jnp.where(is_beta, jax.nn.sigmoid(x), gcum))
    cumt_ref[0] = cum.T[:SUBLANES, :]


def _gate_prep(small, B, S, fox_fb, gdn_a_log, gdn_dt_bias):
    pad = lambda v, at: jnp.zeros((1, SMALL_WIDTH), jnp.float32).at[0, at:at + v.shape[0]].set(v)
    bias = pad(fox_fb, LANE_FOX) + pad(gdn_dt_bias, LANE_DECAY)
    alog = pad(gdn_a_log, LANE_DECAY)
    return pl.pallas_call(
        _gate_prep_kernel,
        grid=(B, S // SEQ_TILE),
        in_specs=[pl.BlockSpec((1, SEQ_TILE, SMALL_WIDTH), lambda b, s: (b, s, 0)),
                  pl.BlockSpec((1, SMALL_WIDTH), lambda b, s: (0, 0)), pl.BlockSpec((1, SMALL_WIDTH), lambda b, s: (0, 0))],
        out_specs=[pl.BlockSpec((1, SEQ_TILE, SMALL_WIDTH), lambda b, s: (b, s, 0)),
                   pl.BlockSpec((1, SUBLANES, SEQ_TILE), lambda b, s: (b, 0, s))],
        out_shape=[jax.ShapeDtypeStruct((B, S, SMALL_WIDTH), jnp.float32), jax.ShapeDtypeStruct((B, SUBLANES, S), jnp.float32)],
        scratch_shapes=[pltpu.VMEM((1, SMALL_WIDTH), jnp.float32)],
        compiler_params=pltpu.CompilerParams(dimension_semantics=("arbitrary", "arbitrary"), vmem_limit_bytes=VMEM_LIMIT_BYTES),
        name="gate_prep",
    )(small.reshape(B, S, SMALL_WIDTH), bias, alog)


ATT_Q = 256
ATT_K = 1024
HEADS_PER_BLOCK = LANES // FOX_HEAD_DIM


def _fox_kernel(q_ref, k_ref, v_ref, cumt_ref, o_ref):
    bf16 = jnp.bfloat16
    hp = pl.program_id(1)
    qi = pl.program_id(2)
    q = q_ref[0] * jnp.asarray(FOX_HEAD_DIM ** -0.5, bf16)
    lane = lax.broadcasted_iota(jnp.int32, (ATT_Q, LANES), 1)
    heads =[hp * HEADS_PER_BLOCK + a for a in range(HEADS_PER_BLOCK)]
    in_head = [(lane >= a * FOX_HEAD_DIM) & (lane < (a + 1) * FOX_HEAD_DIM) for a in range(HEADS_PER_BLOCK)]
    q_head = [jnp.where(in_head[a], q, jnp.zeros_like(q)) for a in range(HEADS_PER_BLOCK)]
    n_full = (qi * ATT_Q) // ATT_K
    diag_case = qi - n_full * (ATT_K // ATT_Q)

    def per_lane(vals):
        out = vals[-1]
        for a in range(HEADS_PER_BLOCK - 2, -1, -1):
            out = jnp.where(in_head[a], vals[a], out)
        return out

    def block(j, carry, width, masked):
        stats, acc = carry
        rows = pl.ds(pl.multiple_of(j * ATT_K, ATT_K), width)
        sub_k = lax.broadcasted_iota(jnp.int32, (SUBLANES, width), 0)
        k = k_ref[0, rows, :]
        v = v_ref[0, rows, :]
        cum_k = cumt_ref[0, :, rows]
        scores = [lax.dot_general(q_head[a], k, (((1,), (1,)), ((), ())), preferred_element_type=jnp.float32)
                  for a in range(HEADS_PER_BLOCK)]
        new_stats, alphas, probs = [], [], []
        for a in range(HEADS_PER_BLOCK):
            m, l = stats[a]
            s = scores[a] - jnp.sum(jnp.where(sub_k == heads[a], cum_k, 0.0), axis=0, keepdims=True)
            if masked:
                visible = (lax.broadcasted_iota(jnp.int32, (ATT_Q, width), 1)
                           <= lax.broadcasted_iota(jnp.int32, (ATT_Q, width), 0) + (width - ATT_Q))
                s = jnp.where(visible, s, -jnp.inf)
            m_new = jnp.maximum(m, jnp.max(s, axis=1, keepdims=True))
            alpha = jnp.exp(m - m_new)
            p = jnp.exp(s - m_new)
            new_stats.append((m_new, alpha * l + jnp.sum(p, axis=1, keepdims=True)))
            alphas.append(alpha)
            probs.append(p.astype(bf16))
        pvs = [jnp.dot(p, v, preferred_element_type=jnp.float32) for p in probs]
        return tuple(new_stats), per_lane(alphas) * acc + per_lane(pvs)

    stat0 = (jnp.full((ATT_Q, 1), -jnp.inf, jnp.float32), jnp.zeros((ATT_Q, 1), jnp.float32))
    init = ((stat0,) * HEADS_PER_BLOCK, jnp.zeros((ATT_Q, LANES), jnp.float32))
    carry = lax.fori_loop(0, n_full, functools.partial(block, width=ATT_K, masked=False), init)
    diagonal = [functools.partial(block, n_full, width=w, masked=True) for w in range(ATT_Q, ATT_K + 1, ATT_Q)]
    stats, acc = lax.switch(diag_case, diagonal, carry)
    o_ref[0] = acc / per_lane([l for _, l in stats])


def _fox_attention(fox_qkv, cumt, B, S):
    qkv = fox_qkv.reshape(B, S, 3 * FOX_WIDTH)
    blocks = FOX_WIDTH // LANES
    out = pl.pallas_call(
        _fox_kernel,
        grid=(B, blocks, S // ATT_Q),
        in_specs=[pl.BlockSpec((1, ATT_Q, LANES), lambda b, hp, qi: (b, qi, hp)),
                  pl.BlockSpec((1, S, LANES), lambda b, hp, qi: (b, 0, blocks + hp)),
                  pl.BlockSpec((1, S, LANES), lambda b, hp, qi: (b, 0, 2 * blocks + hp)),
                  pl.BlockSpec((1, SUBLANES, S), lambda b, hp, qi: (b, 0, 0))],
        out_specs=pl.BlockSpec((1, ATT_Q, LANES), lambda b, hp, qi: (b, qi, hp)),
        out_shape=jax.ShapeDtypeStruct((B, S, FOX_WIDTH), jnp.float32),
        compiler_params=pltpu.CompilerParams(dimension_semantics=("arbitrary",) * 3, vmem_limit_bytes=VMEM_LIMIT_BYTES),
        name="fox_attention",
    )(qkv, qkv, qkv, cumt)
    return out.reshape(B * S, FOX_WIDTH)


POOL_HALO = SUBLANES * 2


def _shift_rows(x, k):
    return pltpu.roll(x, k, 0)


def _pool_kernel(x_ref, w_ref, scale_ref, o_ref, halo_ref):
    @pl.when(pl.program_id(1) == 0)
    def _():
        halo_ref[...] = jnp.zeros_like(halo_ref)

    s_blk = pl.program_id(1)
    rows = POOL_HALO + SEQ_TILE
    row = lax.broadcasted_iota(jnp.int32, (rows, POOL_GROUP_DIM), 0)
    t_abs = s_blk * SEQ_TILE + row - POOL_HALO
    for g, window in enumerate(POOL_WINDOWS):
        cols = slice(g * POOL_GROUP_DIM, (g + 1) * POOL_GROUP_DIM)
        x = x_ref[0, :, cols]
        ext = jnp.concatenate([halo_ref[:, cols], x], axis=0)
        acc = ext
        span = 1
        while span < window:
            acc = acc + jnp.where(row >= span, _shift_rows(acc, span), 0.0)
            span *= 2
        win_sum = acc[POOL_HALO:, :]
        count = jnp.minimum(t_abs[POOL_HALO:, :] + 1, window).astype(jnp.float32)
        pooled = win_sum / count - x
        mixed = jnp.dot(pooled.astype(jnp.bfloat16), w_ref[g], preferred_element_type=jnp.float32)
        o_ref[0, :, cols] = mixed * scale_ref[:, cols]
    halo_ref[...] = x_ref[0, SEQ_TILE - POOL_HALO:, :]


def _multiscale_pool(pool_in, w_pool, pool_scale, B, S):
    out = pl.pallas_call(
        _pool_kernel,
        grid=(B, S // SEQ_TILE),
        in_specs=[pl.BlockSpec((1, SEQ_TILE, POOL_WIDTH), lambda b, s: (b, s, 0)),
                  pl.BlockSpec(w_pool.shape, lambda b, s: (0, 0, 0)), pl.BlockSpec((1, POOL_WIDTH), lambda b, s: (0, 0))],
        out_specs=pl.BlockSpec((1, SEQ_TILE, POOL_WIDTH), lambda b, s: (b, s, 0)),
        out_shape=jax.ShapeDtypeStruct((B, S, POOL_WIDTH), jnp.float32),
        scratch_shapes=[pltpu.VMEM((POOL_HALO, POOL_WIDTH), jnp.float32)],
        compiler_params=pltpu.CompilerParams(dimension_semantics=("arbitrary", "arbitrary"), vmem_limit_bytes=VMEM_LIMIT_BYTES),
        name="pool",
    )(pool_in.reshape(B, S, POOL_WIDTH), w_pool.astype(jnp.bfloat16), pool_scale.reshape(1, POOL_WIDTH))
    return out.reshape(B * S, POOL_WIDTH)


GDN_TILE_CHUNKS = SEQ_TILE // GDN_CHUNK
GDN_HALO = SUBLANES


def _l2norm(x):
    return x * lax.rsqrt(jnp.sum(x * x, axis=-1, keepdims=True) + L2_EPS)


def _silu(x):
    return x * jax.nn.sigmoid(x)


def _bdot(a, b, dims):
    return lax.dot_general(a.astype(jnp.bfloat16), b.astype(jnp.bfloat16), ((dims[0], dims[1]), ((0,), (0,))),
                           preferred_element_type=jnp.float32)


def _bdot3(a, b):
    bf16 = jnp.bfloat16
    a_hi, b_hi = a.astype(bf16), b.astype(bf16)
    a_lo = (a - a_hi.astype(jnp.float32)).astype(bf16)
    b_lo = (b - b_hi.astype(jnp.float32)).astype(bf16)
    dot = lambda x, y: lax.dot_general(x, y, (((2,), (1,)), ((0,), (0,))), preferred_element_type=jnp.float32)
    return dot(a_hi, b_hi) + dot(a_hi, b_lo) + dot(a_lo, b_hi)


def _unit_lower_inverse(low):
    c = low.shape[-1]
    eye = (lax.broadcasted_iota(jnp.int32, (c, c), 0) == lax.broadcasted_iota(jnp.int32, (c, c), 1)).astype(jnp.float32)
    inv = eye[None] - low
    power = low
    span = 2
    while span < c:
        power = _bdot3(power, power)
        inv = inv + _bdot3(inv, power)
        span *= 2
    return inv


def _gdn_kernel(qkv_ref, z_ref, aux_ref, convw_ref, norm_ref, o_ref, halo_ref, state_ref):
    @pl.when(pl.program_id(1) == 0)
    def _():
        halo_ref[...] = jnp.zeros_like(halo_ref)
        state_ref[...] = jnp.zeros_like(state_ref)

    nc, c, d = GDN_TILE_CHUNKS, GDN_CHUNK, GDN_HEAD_DIM
    x = qkv_ref[0]
    ext = jnp.concatenate([halo_ref[...], x], axis=0)
    conv = ext * convw_ref[GDN_CONV - 1:GDN_CONV, :]
    for k in range(1, GDN_CONV):
        conv = conv + pltpu.roll(ext, k, 0) * convw_ref[GDN_CONV - 1 - k:GDN_CONV - k, :]
    act = _silu(conv[GDN_HALO:, :])
    halo_ref[...] = x[SEQ_TILE - GDN_HALO:, :]

    aux = aux_ref[0]
    aux_t = aux.T
    lane = lax.broadcasted_iota(jnp.int32, aux.shape, 1)
    ri = lax.broadcasted_iota(jnp.int32, (c, c), 0)
    ci = lax.broadcasted_iota(jnp.int32, (c, c), 1)
    heads = []
    for hd in range(GDN_HEADS):
        q = _l2norm(act[:, hd * d:(hd + 1) * d]) * (d ** -0.5)
        k = _l2norm(act[:, GDN_WIDTH + hd * d:GDN_WIDTH + (hd + 1) * d])
        v = act[:, 2 * GDN_WIDTH + hd * d:2 * GDN_WIDTH + (hd + 1) * d]
        beta = jnp.sum(jnp.where(lane == LANE_BETA + hd, aux, 0.0), axis=1, keepdims=True)
        g_col = jnp.sum(jnp.where(lane == LANE_DECAY + hd, aux, 0.0), axis=1, keepdims=True)
        g_row = aux_t[LANE_DECAY + hd:LANE_DECAY + hd + 1, :]
        diff = jnp.stack([g_col[i * c:(i + 1) * c, :] - g_row[:, i * c:(i + 1) * c] for i in range(nc)])
        k_beta = k * beta
        q3, k3 = q.reshape(nc, c, d), k.reshape(nc, c, d)
        g3 = g_col.reshape(nc, c, 1)
        g_last = g3[:, c - 1:c, :]
        heads.append(dict(
            low=_bdot(k_beta.reshape(nc, c, d), k3, ((2,), (2,))) * jnp.exp(jnp.where((ri > ci)[None], diff, -jnp.inf)),
            v_beta=(v * beta).reshape(nc, c, d),
            k_beta_dec=(k_beta * jnp.exp(g_col)).reshape(nc, c, d),
            a_intra=_bdot(q3, k3, ((2,), (2,))) * jnp.exp(jnp.where((ri >= ci)[None], diff, -jnp.inf)),
            q_dec=(q * jnp.exp(g_col)).reshape(nc, c, d),
            k_dec=k3 * jnp.exp(g_last - g3),
            g_last=g_last))
    t_inv = _unit_lower_inverse(jnp.concatenate([h["low"] for h in heads], axis=0))
    for hd, h in enumerate(heads):
        t_h = t_inv[hd * nc:(hd + 1) * nc]
        h["u"] = _bdot(t_h, h["v_beta"], ((2,), (1,)))
        h["w"] = _bdot(t_h, h["k_beta_dec"], ((2,), (1,)))
    bf16 = jnp.bfloat16
    states = [state_ref[hd] for hd in range(GDN_HEADS)]
    outs = [[] for _ in range(GDN_HEADS)]
    for i in range(nc):
        for hd, h in enumerate(heads):
            state_b = states[hd].astype(bf16)
            v_new = h["u"][i] - jnp.dot(h["w"][i].astype(bf16), state_b, preferred_element_type=jnp.float32)
            v_new_b = v_new.astype(bf16)
            outs[hd].append(jnp.dot(h["q_dec"][i].astype(bf16), state_b, preferred_element_type=jnp.float32)
                            + jnp.dot(h["a_intra"][i].astype(bf16), v_new_b, preferred_element_type=jnp.float32))
            states[hd] = states[hd] * jnp.exp(h["g_last"][i]) + lax.dot_general(
                h["k_dec"][i].astype(bf16), v_new_b, (((0,), (0,)), ((), ())), preferred_element_type=jnp.float32)
    for hd in range(GDN_HEADS):
        cols = slice(hd * d, (hd + 1) * d)
        state_ref[hd] = states[hd]
        o = jnp.concatenate(outs[hd], axis=0)
        o = o * lax.rsqrt(jnp.mean(o * o, axis=-1, keepdims=True) + RMS_EPS) * norm_ref[...]
        o_ref[0, :, cols] = o * _silu(z_ref[0, :, cols])


def _gated_deltanet(gdn_qkv, gdn_z, aux, conv_w, norm_g, B, S):
    width = 3 * GDN_WIDTH
    out = pl.pallas_call(
        _gdn_kernel,
        grid=(B, S // SEQ_TILE),
        in_specs=[pl.BlockSpec((1, SEQ_TILE, width), lambda b, s: (b, s, 0)),
                  pl.BlockSpec((1, SEQ_TILE, GDN_WIDTH), lambda b, s: (b, s, 0)),
                  pl.BlockSpec((1, SEQ_TILE, SMALL_WIDTH), lambda b, s: (b, s, 0)),
                  pl.BlockSpec((GDN_CONV, width), lambda b, s: (0, 0)),
                  pl.BlockSpec((1, GDN_HEAD_DIM), lambda b, s: (0, 0))],
        out_specs=pl.BlockSpec((1, SEQ_TILE, GDN_WIDTH), lambda b, s: (b, s, 0)),
        out_shape=jax.ShapeDtypeStruct((B, S, GDN_WIDTH), jnp.float32),
        scratch_shapes=[pltpu.VMEM((GDN_HALO, width), jnp.float32),
                        pltpu.VMEM((GDN_HEADS, GDN_HEAD_DIM, GDN_HEAD_DIM), jnp.float32)],
        compiler_params=pltpu.CompilerParams(dimension_semantics=("arbitrary", "arbitrary"), vmem_limit_bytes=VMEM_LIMIT_BYTES),
        name="gated_deltanet",
    )(gdn_qkv.reshape(B, S, width), gdn_z.reshape(B, S, GDN_WIDTH), aux, conv_w.reshape(GDN_CONV, width),
      norm_g.reshape(1, GDN_HEAD_DIM))
    return out.reshape(B * S, GDN_WIDTH)


def _mixer_branches(proj, B, S, fox_fb, pool_w, pool_scale, gdn_conv, gdn_a_log, gdn_dt_bias, gdn_norm):
    aux, cumt = _gate_prep(proj["small"], B, S, fox_fb, gdn_a_log, gdn_dt_bias)
    branch_a = _fox_attention(proj["fox_qkv"], cumt, B, S)
    branch_b = _multiscale_pool(proj["pool_in"], pool_w, pool_scale, B, S)
    branch_c = _gated_deltanet(proj["gdn_qkv"], proj["gdn_z"], aux, gdn_conv, gdn_norm, B, S)
    return branch_a, branch_b, branch_c


def kernel(x, p, w_in, fox_fb, pool_w, pool_scale, gdn_conv, gdn_a_log, gdn_dt_bias, gdn_norm, w_branch, w_out, ln1_g, ln1_b, peer_wq, peer_k1, peer_k2, peer_u, peer_v, ple_gate, ple_proj, ln2_g, ln2_b):
    B, S, D = x.shape
    h2 = x.reshape(B * S, D)
    for i in range(DEPTH):
        proj = _inproj(h2, _rearrange_w_in(w_in[i]))
        br_a, br_b, br_c = _mixer_branches(proj, B, S, fox_fb[i], pool_w[i], pool_scale[i], gdn_conv[i], gdn_a_log[i],
                                           gdn_dt_bias[i], gdn_norm[i])
        h2 = _merge(h2, br_a, br_b, br_c, proj["gates"], w_branch[i], w_out[i], ln1_g[i], ln1_b[i])
        idx_t, wts_t = _peer_select(h2, peer_wq[i], peer_k1[i], peer_k2[i])
        ffn = _peer_retrieve(h2, idx_t, wts_t, _pack_table(peer_u[i]), _pack_table(peer_v[i]))
        h2 = _ple_ln(h2, p[i].reshape(B * S, -1), ffn, ple_gate[i], ple_proj[i], ln2_g[i], ln2_b[i])
    return h2.reshape(B, S, D)
```
